```python
import jax, jax.numpy as jnp
from jax import lax
import numpy as np

D_MODEL = 1024
BATCH = 8
SEQ = 4096
DEPTH = 4

GM_WIDTH = 512
GM_GROUPS = 4
GM_GROUP_DIM = GM_WIDTH // GM_GROUPS
GM_CHUNK = 128
N_Q_HEADS = 8
N_KV_HEADS = 2
HEAD_DIM = 64
ATT_WIDTH = N_Q_HEADS * HEAD_DIM
KV_WIDTH = N_KV_HEADS * HEAD_DIM
WINDOW = 128
ATTN_BLOCK = 128
ROPE_THETA = 10000.0
MAX_POS_OFFSET = 1024
MASK_VALUE = -1e30
HG_HEADS = 4
HG_EXPAND = 128
HG_HEAD_V = 128
HG_KEY_WIDTH = HG_HEADS * HG_EXPAND
HG_WIDTH = HG_HEADS * HG_HEAD_V
HG_CHUNK = 64
HG_MIN_FORGET = 1e-6
N_BRANCH = 3
D_FF = 2816
CONV_W = 3
EPS = 1e-6

IN_SPLITS = (GM_WIDTH, GM_WIDTH, ATT_WIDTH, KV_WIDTH, KV_WIDTH,
             HG_KEY_WIDTH, HG_KEY_WIDTH, HG_WIDTH, HG_WIDTH, N_BRANCH * D_MODEL)
IN_WIDTH = sum(IN_SPLITS)

kernel_name = "hybrid_gated_parallel_mixer_block"


def rms_norm(x, g):
    xf = x.astype(jnp.float32)
    y = xf * lax.rsqrt(jnp.mean(xf * xf, axis=-1, keepdims=True) + EPS)
    return (y * g.astype(jnp.float32)).astype(x.dtype)


def layer_norm(x, g, b):
    xf = x.astype(jnp.float32)
    mu = jnp.mean(xf, axis=-1, keepdims=True)
    var = jnp.mean(jnp.square(xf - mu), axis=-1, keepdims=True)
    y = (xf - mu) * lax.rsqrt(var + EPS)
    return (y * g.astype(jnp.float32) + b.astype(jnp.float32)).astype(x.dtype)


def rope(x, pos):
    half = HEAD_DIM // 2
    inv = ROPE_THETA ** (-jnp.arange(half, dtype=jnp.float32) / half)
    ang = pos.astype(jnp.float32)[..., None] * inv
    cos = jnp.cos(ang)[:, :, None, :]
    sin = jnp.sin(ang)[:, :, None, :]
    xf = x.astype(jnp.float32)
    x1, x2 = xf[..., :half], xf[..., half:]
    return jnp.concatenate([x1 * cos - x2 * sin, x2 * cos + x1 * sin], axis=-1).astype(x.dtype)


def chunked_spatial_gating(u, v, ln_g, ln_b, w_s, b_s):
    B, S, _ = u.shape
    nc = S // GM_CHUNK
    vn = layer_norm(v, ln_g, ln_b).reshape(B, nc, GM_CHUNK, GM_GROUPS, GM_GROUP_DIM)
    causal = jnp.tril(jnp.ones((GM_CHUNK, GM_CHUNK), dtype=bool))
    ws = jnp.where(causal[None], w_s, 0.0).astype(u.dtype)
    s = jnp.einsum('gts,bcsgd->bctgd', ws, vn) + b_s.T.astype(u.dtype)[None, None, :, :, None]
    return u * s.reshape(B, S, GM_WIDTH)


def sliding_window_attention(q, k, v, sinks):
    B, S = q.shape[:2]
    nb = S // ATTN_BLOCK
    G = N_Q_HEADS // N_KV_HEADS
    qb = q.reshape(B, nb, ATTN_BLOCK, N_KV_HEADS, G, HEAD_DIM)

    def with_prev(t):
        tb = t.reshape(B, nb, ATTN_BLOCK, N_KV_HEADS, HEAD_DIM)
        prev = jnp.pad(tb[:, :-1], ((0, 0), (1, 0), (0, 0), (0, 0), (0, 0)))
        return jnp.concatenate([prev, tb], axis=2)

    kk, vv = with_prev(k), with_prev(v)
    scores = jnp.einsum('bnqhgd,bnkhd->bnhgqk', qb, kk).astype(jnp.float32) * (HEAD_DIM ** -0.5)
    qi = jnp.arange(ATTN_BLOCK)[:, None]
    kj = jnp.arange(2 * ATTN_BLOCK)[None, :]
    delta = qi + ATTN_BLOCK - kj
    band = (delta >= 0) & (delta < WINDOW)
    blk = jnp.arange(nb)[:, None, None]
    valid = band[None] & ((blk > 0) | (kj >= ATTN_BLOCK)[None])
    scores = jnp.where(valid[None, :, None, None], scores, MASK_VALUE)
    sink = sinks.astype(jnp.float32).reshape(N_KV_HEADS, G)[None, None, :, :, None, None]
    m = jnp.maximum(jnp.max(scores, axis=-1, keepdims=True), sink)
    p = jnp.exp(scores - m)
    probs = p / (jnp.sum(p, axis=-1, keepdims=True) + jnp.exp(sink - m))
    out = jnp.einsum('bnhgqk,bnkhd->bnqhgd', probs.astype(v.dtype), vv)
    return out.reshape(B, S, ATT_WIDTH)


def hgrn2(q, f_logit, i, g, lb, gnorm):
    B, S, _ = q.shape
    nc = S // HG_CHUNK
    f32 = jnp.float32
    fl = f_logit.astype(f32)
    lbf = lb.astype(f32)
    forget = lbf + (1.0 - lbf) * jax.nn.sigmoid(fl)
    log_f = jnp.log(jnp.maximum(forget, HG_MIN_FORGET))
    k = (1.0 - lbf) * jax.nn.sigmoid(-fl)
    qf = jax.nn.silu(q.astype(f32)) * (HG_EXPAND ** -0.5)
    vf = i.astype(f32)

    def to_chunks(t, d):
        return t.reshape(B, nc, HG_CHUNK, HG_HEADS, d).transpose(1, 0, 2, 3, 4)

    qs, ks, lfs = to_chunks(qf, HG_EXPAND), to_chunks(k, HG_EXPAND), to_chunks(log_f, HG_EXPAND)
    vs = to_chunks(vf, HG_HEAD_V)
    causal = jnp.tril(jnp.ones((HG_CHUNK, HG_CHUNK), dtype=bool))[None, :, :, None, None]

    def step(state, inp):
        qc, kc, vc, lfc = inp
        b = jnp.cumsum(lfc, axis=1)
        o_inter = jnp.einsum('bthk,bhkv->bthv', qc * jnp.exp(b), state)
        diff = jnp.where(causal, b[:, :, None] - b[:, None, :], 0.0)
        decay = jnp.where(causal, jnp.exp(diff), 0.0)
        att = jnp.einsum('bthk,btshk,bshk->bths', qc, decay, kc)
        o = o_inter + jnp.einsum('bths,bshv->bthv', att, vc)
        b_last = b[:, -1]
        new_state = jnp.exp(b_last)[..., None] * state + jnp.einsum(
            'bshk,bshv->bhkv', kc * jnp.exp(b_last[:, None] - b), vc)
        return new_state, o

    s0 = jnp.zeros((B, HG_HEADS, HG_EXPAND, HG_HEAD_V), f32)
    _, o = lax.scan(step, s0, (qs, ks, vs, lfs))
    o = o.transpose(1, 0, 2, 3, 4).reshape(B, S, HG_HEADS, HG_HEAD_V)
    o = rms_norm(o, gnorm) * jax.nn.silu(g.astype(f32)).reshape(B, S, HG_HEADS, HG_HEAD_V)
    return o.reshape(B, S, HG_WIDTH).astype(q.dtype)


def hybrid_mixer(h, pos, lb, w_in, gm_ln_g, gm_ln_b, gm_ws, gm_bs, sinks, hg_gnorm,
                 proj_a, proj_b, proj_c, w_out):
    B, S, _ = h.shape
    proj = h @ w_in
    idx = [int(t) for t in np.cumsum(IN_SPLITS)[:-1]]
    gu, gv, aq, ak, av, hq, hf, hi, hg, gate_logits = jnp.split(proj, idx, axis=-1)
    y_a = chunked_spatial_gating(jax.nn.gelu(gu), jax.nn.gelu(gv), gm_ln_g, gm_ln_b, gm_ws, gm_bs)
    q = rope(aq.reshape(B, S, N_Q_HEADS, HEAD_DIM), pos)
    k = rope(ak.reshape(B, S, N_KV_HEADS, HEAD_DIM), pos)
    v = av.reshape(B, S, N_KV_HEADS, HEAD_DIM)
    y_b = sliding_window_attention(q, k, v, sinks)
    y_c = hgrn2(hq, hf, hi, hg, lb, hg_gnorm)
    gates = jax.nn.sigmoid(gate_logits.astype(jnp.float32)).astype(h.dtype).reshape(B, S, N_BRANCH, D_MODEL)
    merged = (gates[:, :, 0] * (y_a @ proj_a) + gates[:, :, 1] * (y_b @ proj_b)
              + gates[:, :, 2] * (y_c @ proj_c))
    return merged @ w_out


def conv_ffn(h, w1, conv_w, conv_b, w2):
    a = h @ w1
    a = lax.conv_general_dilated(
        a, conv_w[:, None, :].astype(a.dtype), window_strides=(1,), padding=[(CONV_W - 1, 0)],
        dimension_numbers=('NWC', 'WIO', 'NWC'), feature_group_count=2 * D_FF) + conv_b
    gate, val = jnp.split(a, 2, axis=-1)
    return (jax.nn.silu(gate) * val) @ w2


def setup_inputs(seed: int = 0) -> dict:
    key = jax.random.key(seed)
    ks = jax.random.split(key, 32)
    f32 = jnp.float32
    L = DEPTH

    def nrm(k, shape, scale):
        return jax.random.normal(k, shape, f32) * scale

    def gain(k, shape):
        return 1.0 + nrm(k, shape, 0.02)

    offset = jax.random.randint(ks[2], (BATCH, 1), 0, MAX_POS_OFFSET, dtype=jnp.int32)
    return {
        "x": nrm(ks[0], (BATCH, SEQ, D_MODEL), 1.0),
        "c": nrm(ks[1], (BATCH, D_MODEL), 1.0),
        "positions": offset + jnp.arange(SEQ, dtype=jnp.int32)[None, :],
        "ada_w": nrm(ks[3], (L, D_MODEL, 6 * D_MODEL), 0.5 * D_MODEL ** -0.5),
        "ada_b": nrm(ks[4], (L, 6 * D_MODEL), 0.02),
        "norm_mix_pre": gain(ks[5], (L, D_MODEL)),
        "norm_mix_post": gain(ks[6], (L, D_MODEL)),
        "norm_ffn_pre": gain(ks[7], (L, D_MODEL)),
        "norm_ffn_post": gain(ks[8], (L, D_MODEL)),
        "w_in": nrm(ks[9], (L, D_MODEL, IN_WIDTH), D_MODEL ** -0.5),
        "gm_ln_g": gain(ks[10], (L, GM_WIDTH)),
        "gm_ln_b": nrm(ks[11], (L, GM_WIDTH), 0.02),
        "gm_ws": nrm(ks[12], (L, GM_GROUPS, GM_CHUNK, GM_CHUNK), GM_CHUNK ** -0.5),
        "gm_bs": gain(ks[13], (L, GM_GROUPS, GM_CHUNK)),
        "attn_sinks": nrm(ks[14], (L, N_Q_HEADS), 0.5),
        "hg_lb_logits": nrm(ks[15], (L, HG_KEY_WIDTH), 0.5),
        "hg_gnorm": gain(ks[16], (L, HG_HEAD_V)),
        "proj_a": nrm(ks[17], (L, GM_WIDTH, D_MODEL), GM_WIDTH ** -0.5),
        "proj_b": nrm(ks[18], (L, ATT_WIDTH, D_MODEL), ATT_WIDTH ** -0.5),
        "proj_c": nrm(ks[19], (L, HG_WIDTH, D_MODEL), HG_WIDTH ** -0.5),
        "w_out": nrm(ks[20], (L, D_MODEL, D_MODEL), D_MODEL ** -0.5),
        "ffn_w1": nrm(ks[21], (L, D_MODEL, 2 * D_FF), D_MODEL ** -0.5),
        "ffn_conv_w": nrm(ks[22], (L, CONV_W, 2 * D_FF), CONV_W ** -0.5),
        "ffn_conv_b": nrm(ks[23], (L, 2 * D_FF), 0.02),
        "ffn_w2": nrm(ks[24], (L, D_FF, D_MODEL), D_FF ** -0.5),
    }


def reference(x, c, positions, ada_w, ada_b, norm_mix_pre, norm_mix_post, norm_ffn_pre, norm_ffn_post,
              w_in, gm_ln_g, gm_ln_b, gm_ws, gm_bs, attn_sinks, hg_lb_logits, hg_gnorm,
              proj_a, proj_b, proj_c, w_out, ffn_w1, ffn_conv_w, ffn_conv_b, ffn_w2):
    p_lb = jax.nn.softmax(hg_lb_logits.astype(jnp.float32), axis=0)
    lb_all = jnp.cumsum(p_lb, axis=0) - p_lb[0]
    c_act = jax.nn.silu(c)
    for l in range(DEPTH):
        mod = c_act @ ada_w[l] + ada_b[l]
        sh1, sc1, g1, sh2, sc2, g2 = jnp.split(mod, 6, axis=-1)
        h = rms_norm(x, norm_mix_pre[l]) * (1.0 + sc1[:, None]) + sh1[:, None]
        y = hybrid_mixer(h, positions, lb_all[l], w_in[l], gm_ln_g[l], gm_ln_b[l], gm_ws[l], gm_bs[l],
                         attn_sinks[l], hg_gnorm[l], proj_a[l], proj_b[l], proj_c[l], w_out[l])
        x = x + g1[:, None] * rms_norm(y, norm_mix_post[l])
        h = rms_norm(x, norm_ffn_pre[l]) * (1.0 + sc2[:, None]) + sh2[:, None]
        y = conv_ffn(h, ffn_w1[l], ffn_conv_w[l], ffn_conv_b[l], ffn_w2[l])
        x = x + g2[:, None] * rms_norm(y, norm_ffn_post[l])
    return x
```

```python
import functools

import jax
import jax.numpy as jnp
import numpy as np
from jax import lax
from jax.experimental import pallas as pl
from jax.experimental.pallas import tpu as pltpu

F32 = jnp.float32
BF16 = jnp.bfloat16

D_MODEL = 1024
DEPTH = 4
GM_WIDTH = 512
GM_GROUPS = 4
N_Q_HEADS = 8
N_KV_HEADS = 2
HEAD_DIM = 64
ATT_WIDTH = N_Q_HEADS * HEAD_DIM
ROPE_THETA = 10000.0
MASK_VALUE = -1e30
HG_HEADS = 4
HG_EXPAND = 128
HG_KEY_WIDTH = HG_HEADS * HG_EXPAND
HG_WIDTH = HG_HEADS * 128
HG_MIN_FORGET = 1e-6
D_FF = 2816
EPS = 1e-6

LANES = 128
BLK = 128
TM_IN = 256
TM_FFN = 256
FF_CHUNK = 256
VMEM_LIMIT = 56 * 1024 * 1024

C_GU, C_GV, C_Q, C_K, C_V, C_HQ, C_HF, C_HI, C_HG, C_GATE, C_END = (
    0, 512, 1024, 1536, 1792, 2048, 2560, 3072, 3584, 4096, 7168)

HG_LEVEL_HALVES = (64, 32, 16, 8)
HG_DIAG = 8
N_PFX = 3 + len(HG_LEVEL_HALVES)


def _hgrn_tables():
    c = BLK
    t = np.arange(c)
    tt = t[:, None]
    j = np.arange(c)[None, :]
    mats = [(j <= tt).astype(np.float32)]
    lvl = np.zeros((c, c), np.int32)
    for li, h in enumerate(HG_LEVEL_HALVES):
        blk = 2 * h
        start = (t // blk) * blk
        refpos = (start + h - 1)[:, None]
        second = ((t - start) >= h)
        p = np.where(second[:, None], (j > refpos) & (j <= tt), (j > tt) & (j <= refpos))
        mats.append(p.astype(np.float32))
        same = (t[:, None] // blk) == (t[None, :] // blk)
        lvl[same & second[:, None] & (~second)[None, :]] = li + 1
    start = (t // HG_DIAG) * HG_DIAG
    refpos = (start + HG_DIAG // 2 - 1)[:, None]
    pq = np.where(tt > refpos, ((j > refpos) & (j <= tt)).astype(np.float32),
                  -((j > tt) & (j <= refpos)).astype(np.float32))
    mats.append(pq)
    same = (t[:, None] // HG_DIAG) == (t[None, :] // HG_DIAG)
    lvl[same & (t[None, :] <= t[:, None])] = len(HG_LEVEL_HALVES) + 1
    mats.append((j > tt).astype(np.float32))
    return np.concatenate(mats, axis=0), lvl


_HG_PFX_NP, _HG_LVL_NP = _hgrn_tables()


def _dot(a, b):
    return jnp.dot(a, b, preferred_element_type=F32)


def _dot_nt(a, b):
    return lax.dot_general(a, b, (((1,), (1,)), ((), ())), preferred_element_type=F32)


def _sigmoid(x):
    return 1.0 / (1.0 + jnp.exp(-x))


def _silu(x):
    return x * _sigmoid(x)


def _gelu_tanh(x):
    return 0.5 * x * (1.0 + jnp.tanh(np.sqrt(2.0 / np.pi).astype(np.float32) * (x + 0.044715 * (x * x * x))))


def _rms(x):
    return x * lax.rsqrt(jnp.mean(x * x, axis=-1, keepdims=True) + EPS)


def _ada_kernel(c_ref, w_ref, b_ref, o_ref):
    c = c_ref[...]
    o_ref[0] = jnp.dot(_silu(c), w_ref[0], preferred_element_type=F32,
                       precision=lax.Precision.HIGHEST) + b_ref[0]


def _ada_mod(c, ada_w, ada_b):
    nb = c.shape[0]
    ncol = ada_w.shape[2] // D_MODEL
    return pl.pallas_call(
        _ada_kernel,
        grid=(DEPTH, ncol),
        in_specs=[pl.BlockSpec((nb, D_MODEL), lambda l, j: (0, 0)),
                  pl.BlockSpec((1, D_MODEL, D_MODEL), lambda l, j: (l, 0, j)),
                  pl.BlockSpec((1, 1, D_MODEL), lambda l, j: (l, 0, j))],
        out_specs=pl.BlockSpec((1, nb, D_MODEL), lambda l, j: (l, 0, j)),
        out_shape=jax.ShapeDtypeStruct((DEPTH, nb, ncol * D_MODEL), F32),
        name="ada_mod",
    )(c, ada_w, ada_b.reshape(DEPTH, 1, -1))


def _lb_kernel(x_ref, o_ref):
    x = x_ref[...]
    e = jnp.exp(x - jnp.max(x, axis=0, keepdims=True))
    p = e / jnp.sum(e, axis=0, keepdims=True)
    run = jnp.zeros_like(p[0:1])
    rows = []
    for l in range(DEPTH):
        run = run + p[l:l + 1]
        rows.append(run - p[0:1])
    o_ref[...] = jnp.concatenate(rows, axis=0)


def _forget_lower_bounds(logits):
    return pl.pallas_call(
        _lb_kernel, out_shape=jax.ShapeDtypeStruct(logits.shape, F32), name="hgrn_lower_bounds",
    )(logits.astype(F32))


def _rope_kernel(pos_ref, inv_ref, sign_ref, cos_ref, sin_ref):
    ang = pos_ref[0] * inv_ref[...]
    cos_ref[0] = jnp.cos(ang)
    sin_ref[0] = jnp.sin(ang) * sign_ref[...]


def _rope_tables(positions):
    nb, s = positions.shape
    half = HEAD_DIM // 2
    inv = ROPE_THETA ** (-jnp.arange(half, dtype=F32) / half)
    inv_t = jnp.tile(inv, LANES // half)[None, :]
    sign = np.where((np.arange(LANES) % HEAD_DIM) < half, -1.0, 1.0).astype(np.float32)[None, :]
    ts = 512
    tab = jax.ShapeDtypeStruct((nb, s, LANES), F32)
    return pl.pallas_call(
        _rope_kernel,
        grid=(nb, s // ts),
        in_specs=[pl.BlockSpec((1, ts, 1), lambda b, i: (b, i, 0)),
                  pl.BlockSpec((1, LANES), lambda b, i: (0, 0)),
                  pl.BlockSpec((1, LANES), lambda b, i: (0, 0))],
        out_specs=[pl.BlockSpec((1, ts, LANES), lambda b, i: (b, i, 0))] * 2,
        out_shape=[tab, tab],
        name="rope_tables",
    )(positions.astype(F32)[..., None], inv_t, jnp.asarray(sign))


def _rope_apply(x, cos, sin_signed, first_half):
    partner = jnp.where(first_half, pltpu.roll(x, LANES - HEAD_DIM // 2, 1), pltpu.roll(x, HEAD_DIM // 2, 1))
    return x * cos + partner * sin_signed


def _in_kernel(x_ref, sh_ref, sc_ref, g_ref, w_ref, cos_ref, sin_ref, lb_ref, lng_ref, lnb_ref,
               uv_ref, qkv_ref, hg_ref, lf_ref, gate_ref):
    x = x_ref[0]
    h = _rms(x) * g_ref[...]
    h = h * (1.0 + sc_ref[0]) + sh_ref[0]
    hb = h.astype(BF16)

    def proj(a, b):
        return _dot(hb, w_ref[:, a:b])

    uv_ref[0, :, 0:GM_WIDTH] = _gelu_tanh(proj(C_GU, C_GV)).astype(BF16)
    v = _gelu_tanh(proj(C_GV, C_Q))
    mu = jnp.mean(v, axis=-1, keepdims=True)
    vc = v - mu
    var = jnp.mean(vc * vc, axis=-1, keepdims=True)
    vn = vc * lax.rsqrt(var + EPS) * lng_ref[...] + lnb_ref[...]
    uv_ref[0, :, GM_WIDTH:2 * GM_WIDTH] = vn.astype(BF16)

    cos = cos_ref[0]
    sin = sin_ref[0]
    lane = lax.broadcasted_iota(jnp.int32, cos.shape, 1)
    first_half = (lane % HEAD_DIM) < (HEAD_DIM // 2)
    qk = proj(C_Q, C_V)
    scale = HEAD_DIM ** -0.5
    for gidx in range((C_V - C_Q) // LANES):
        xg = qk[:, gidx * LANES:(gidx + 1) * LANES]
        r = _rope_apply(xg, cos, sin, first_half)
        if gidx < ATT_WIDTH // LANES:
            r = r * scale
        qkv_ref[0, :, gidx * LANES:(gidx + 1) * LANES] = r.astype(BF16)
    qkv_ref[0, :, C_V - C_Q:C_HQ - C_Q] = proj(C_V, C_HQ).astype(BF16)

    hg_ref[0, :, 0:512] = (_silu(proj(C_HQ, C_HF)) * (HG_EXPAND ** -0.5)).astype(BF16)
    fl = proj(C_HF, C_HI)
    lb = lb_ref[...]
    forget = lb + (1.0 - lb) * _sigmoid(fl)
    lf_ref[0] = jnp.log(jnp.maximum(forget, HG_MIN_FORGET))
    hg_ref[0, :, 512:1024] = ((1.0 - lb) * _sigmoid(-fl)).astype(BF16)
    hg_ref[0, :, 1024:1536] = proj(C_HI, C_HG).astype(BF16)
    hg_ref[0, :, 1536:2048] = _silu(proj(C_HG, C_GATE)).astype(BF16)

    for gidx in range(3 * D_MODEL // 512):
        a = C_GATE + gidx * 512
        gate_ref[0, :, gidx * 512:(gidx + 1) * 512] = _sigmoid(proj(a, a + 512)).astype(BF16)


def _const_spec(shape):
    nd = len(shape)
    return pl.BlockSpec(shape, lambda b, i: (0,) * nd, pipeline_mode=pl.Buffered(1))


def _in_call(x, sh, sc, g, w, cos_t, sin_t, lb, lng, lnb):
    nb, s, _ = x.shape
    tm = TM_IN

    def tok(width):
        return pl.BlockSpec((1, tm, width), lambda b, i: (b, i, 0))

    def per_batch(width):
        return pl.BlockSpec((1, 1, width), lambda b, i: (b, 0, 0))

    def out(width, dt):
        return jax.ShapeDtypeStruct((nb, s, width), dt)

    return pl.pallas_call(
        _in_kernel,
        grid=(nb, s // tm),
        in_specs=[tok(D_MODEL), per_batch(D_MODEL), per_batch(D_MODEL), _const_spec((1, D_MODEL)),
                  _const_spec((D_MODEL, C_END)), tok(LANES), tok(LANES),
                  _const_spec((1, HG_KEY_WIDTH)), _const_spec((1, GM_WIDTH)), _const_spec((1, GM_WIDTH))],
        out_specs=[tok(1024), tok(1024), tok(2048), tok(512), tok(3072)],
        out_shape=[out(1024, BF16), out(1024, BF16), out(2048, BF16), out(512, F32), out(3072, BF16)],
        compiler_params=pltpu.CompilerParams(
            dimension_semantics=("parallel", "parallel"), vmem_limit_bytes=VMEM_LIMIT),
        name="mixer_in",
    )(x, sh, sc, g, w, cos_t, sin_t, lb, lng, lnb)


def _spatial_gating(uv, ws_ref, bs_ref):
    row = lax.broadcasted_iota(jnp.int32, (BLK, BLK), 0)
    col = lax.broadcasted_iota(jnp.int32, (BLK, BLK), 1)
    causal = col <= row
    outs = []
    for g in range(GM_GROUPS):
        sl = slice(g * LANES, (g + 1) * LANES)
        w = jnp.where(causal, ws_ref[g], 0.0).astype(BF16)
        s = _dot(w, uv[:, GM_WIDTH + g * LANES:GM_WIDTH + (g + 1) * LANES]) + bs_ref[g]
        outs.append(uv[:, sl].astype(F32) * s)
    return jnp.concatenate(outs, axis=1)


def _window_attention(qkv, kprev_ref, vprev_ref, sink_ref, step):
    qi = lax.broadcasted_iota(jnp.int32, (BLK, 2 * BLK), 0)
    kj = lax.broadcasted_iota(jnp.int32, (BLK, 2 * BLK), 1)
    first_key = jnp.where(step > 0, 0, BLK)
    valid = (kj > qi) & (kj <= qi + BLK) & (kj >= first_key)
    lane = lax.broadcasted_iota(jnp.int32, (2 * BLK, LANES), 1)
    low = lane < HEAD_DIM
    lane_o = lax.broadcasted_iota(jnp.int32, (BLK, LANES), 1)
    low_o = lane_o < HEAD_DIM
    pairs_per_kv = (N_Q_HEADS // N_KV_HEADS) // 2
    outs = []
    for j in range(N_KV_HEADS):
        k_cur = qkv[:, ATT_WIDTH + j * LANES:ATT_WIDTH + (j + 1) * LANES]
        v_cur = qkv[:, ATT_WIDTH + (N_KV_HEADS + j) * LANES:ATT_WIDTH + (N_KV_HEADS + j + 1) * LANES]
        kd = jnp.concatenate([kprev_ref[j], k_cur], axis=0)
        vd = jnp.concatenate([vprev_ref[j], v_cur], axis=0)
        zero = jnp.zeros_like(kd)
        k_sel = (jnp.where(low, kd, zero), jnp.where(low, zero, kd))
        for pp in range(pairs_per_kv):
            pair = j * pairs_per_kv + pp
            qp = qkv[:, pair * LANES:(pair + 1) * LANES]
            halves = []
            for par in range(2):
                sink = sink_ref[2 * pair + par]
                s = jnp.where(valid, _dot_nt(qp, k_sel[par]), MASK_VALUE)
                m = jnp.maximum(jnp.max(s, axis=-1, keepdims=True), sink)
                p = jnp.exp(s - m)
                den = jnp.sum(p, axis=-1, keepdims=True) + jnp.exp(sink - m)
                halves.append(_dot(p.astype(BF16), vd) * (1.0 / den))
            outs.append(jnp.where(low_o, halves[0], halves[1]))
        kprev_ref[j] = k_cur
        vprev_ref[j] = v_cur
    return jnp.concatenate(outs, axis=1)


def _hgrn2(hg, lf, pfx_ref, lvl_ref, gn_ref, state_ref):
    lvl = lvl_ref[...]
    n_lv = len(HG_LEVEL_HALVES)
    outs = []
    for hd in range(HG_HEADS):
        sl = slice(hd * LANES, (hd + 1) * LANES)
        q = hg[:, sl].astype(F32)
        k = hg[:, 512 + hd * LANES:512 + (hd + 1) * LANES].astype(F32)
        v = hg[:, 1024 + hd * LANES:1024 + (hd + 1) * LANES]
        gs = hg[:, 1536 + hd * LANES:1536 + (hd + 1) * LANES].astype(F32)
        lfh = lf[:, sl]
        hi = lfh.astype(BF16)
        lo = (lfh - hi.astype(F32)).astype(BF16)
        e2 = _dot(pfx_ref[...], jnp.concatenate([hi, lo], axis=1))
        ex = e2[:, :LANES] + e2[:, LANES:]
        b = ex[0:BLK]
        st = state_ref[hd]
        o = _dot_nt((q * jnp.exp(b)).astype(BF16), st.astype(BF16))
        att = jnp.zeros((BLK, BLK), F32)
        for li in range(n_lv):
            e = jnp.exp(ex[(li + 1) * BLK:(li + 2) * BLK])
            a = _dot_nt((q * e).astype(BF16), (k * e).astype(BF16))
            att = jnp.where(lvl == li + 1, a, att)
        xd = ex[(n_lv + 1) * BLK:(n_lv + 2) * BLK]
        a = _dot_nt((q * jnp.exp(xd)).astype(BF16), (k * jnp.exp(-xd)).astype(BF16))
        att = jnp.where(lvl == n_lv + 1, a, att)
        o = o + _dot(att.astype(BF16), v)
        k_end = (k * jnp.exp(ex[(n_lv + 2) * BLK:(n_lv + 3) * BLK])).astype(BF16)
        vt = v.astype(F32).T.astype(BF16)
        state_ref[hd] = st * jnp.exp(b[BLK - 1:BLK]) + _dot(vt, k_end)
        outs.append(_rms(o) * gn_ref[...] * gs)
    return jnp.concatenate(outs, axis=1)


def _mix_kernel(sink_ref, x_ref, uv_ref, qkv_ref, hg_ref, lf_ref, gate_ref, g1_ref, gpost_ref,
                ws_ref, bs_ref, pfx_ref, lvl_ref, gn_ref, pa_ref, pb_ref, pc_ref, wo_ref,
                o_ref, kprev_ref, vprev_ref, state_ref):
    step = pl.program_id(1)

    @pl.when(step == 0)
    def _():
        kprev_ref[...] = jnp.zeros_like(kprev_ref)
        vprev_ref[...] = jnp.zeros_like(vprev_ref)
        state_ref[...] = jnp.zeros_like(state_ref)

    y_a = _spatial_gating(uv_ref[0], ws_ref, bs_ref)
    y_b = _window_attention(qkv_ref[0], kprev_ref, vprev_ref, sink_ref, step)
    y_c = _hgrn2(hg_ref[0], lf_ref[0], pfx_ref, lvl_ref, gn_ref, state_ref)
    gates = gate_ref[0]
    merged = (gates[:, 0:D_MODEL].astype(F32) * _dot(y_a.astype(BF16), pa_ref[...])
              + gates[:, D_MODEL:2 * D_MODEL].astype(F32) * _dot(y_b.astype(BF16), pb_ref[...])
              + gates[:, 2 * D_MODEL:3 * D_MODEL].astype(F32) * _dot(y_c.astype(BF16), pc_ref[...]))
    y = _dot(merged.astype(BF16), wo_ref[...])
    o_ref[0] = x_ref[0] + g1_ref[0] * (_rms(y) * gpost_ref[...])


def _mix_call(sinks, x, uv, qkv, hg, lf, gates, g1, gpost, ws, bs_b, pfx, lvl, gn, pa, pb, pc, wo):
    nb, s, _ = x.shape

    def tok(width):
        return pl.BlockSpec((1, BLK, width), lambda b, i: (b, i, 0))

    return pl.pallas_call(
        _mix_kernel,
        grid=(nb, s // BLK),
        in_specs=[pl.BlockSpec(memory_space=pltpu.SMEM),
                  tok(D_MODEL), tok(1024), tok(1024), tok(2048), tok(512), tok(3072),
                  pl.BlockSpec((1, 1, D_MODEL), lambda b, i: (b, 0, 0)), _const_spec((1, D_MODEL)),
                  _const_spec((GM_GROUPS, BLK, BLK)), _const_spec((GM_GROUPS, BLK, BLK)),
                  _const_spec((N_PFX * BLK, BLK)), _const_spec((BLK, BLK)), _const_spec((1, LANES)),
                  _const_spec((GM_WIDTH, D_MODEL)), _const_spec((ATT_WIDTH, D_MODEL)),
                  _const_spec((HG_WIDTH, D_MODEL)), _const_spec((D_MODEL, D_MODEL))],
        out_specs=tok(D_MODEL),
        out_shape=jax.ShapeDtypeStruct(x.shape, F32),
        scratch_shapes=[pltpu.VMEM((N_KV_HEADS, BLK, LANES), BF16),
                        pltpu.VMEM((N_KV_HEADS, BLK, LANES), BF16),
                        pltpu.VMEM((HG_HEADS, LANES, HG_EXPAND), F32)],
        compiler_params=pltpu.CompilerParams(
            dimension_semantics=("parallel", "arbitrary"), vmem_limit_bytes=VMEM_LIMIT),
        name="mixer_core",
    )(sinks, x, uv, qkv, hg, lf, gates, g1, gpost, ws, bs_b, pfx, lvl, gn, pa, pb, pc, wo)


def _ffn_kernel(x_ref, sh_ref, sc_ref, gate_ref, gpre_ref, gpost_ref, w1_ref, cw_ref, cb_ref, w2_ref,
                o_ref, halo_ref):
    step = pl.program_id(1)

    @pl.when(step == 0)
    def _():
        halo_ref[...] = jnp.zeros_like(halo_ref)

    x = x_ref[0]
    tm = x.shape[0]
    h = _rms(x) * gpre_ref[...]
    hb = (h * (1.0 + sc_ref[0]) + sh_ref[0]).astype(BF16)
    row = lax.broadcasted_iota(jnp.int32, (tm, FF_CHUNK), 0)

    def conv(col):
        sl = slice(col, col + FF_CHUNK)
        a = _dot(hb, w1_ref[:, sl])
        halo = halo_ref[:, sl]
        a1 = jnp.where(row >= 1, pltpu.roll(a, 1, 0), halo[7:8])
        a2 = jnp.where(row >= 2, pltpu.roll(a, 2, 0), jnp.where(row == 1, halo[7:8], halo[6:7]))
        halo_ref[:, sl] = a[tm - 8:tm]
        cw = cw_ref[:, sl]
        return cw[0:1] * a2 + cw[1:2] * a1 + cw[2:3] * a + cb_ref[:, sl]

    y = jnp.zeros((tm, D_MODEL), F32)
    for c in range(D_FF // FF_CHUNK):
        gate = conv(c * FF_CHUNK)
        val = conv(D_FF + c * FF_CHUNK)
        hid = (_silu(gate) * val).astype(BF16)
        y = y + _dot(hid, w2_ref[c * FF_CHUNK:(c + 1) * FF_CHUNK, :])
    o_ref[0] = x + gate_ref[0] * (_rms(y) * gpost_ref[...])


def _ffn_call(x, sh, sc, g2, gpre, gpost, w1, cw, cb, w2):
    nb, s, _ = x.shape
    tm = TM_FFN
    tok = pl.BlockSpec((1, tm, D_MODEL), lambda b, i: (b, i, 0))
    per_batch = pl.BlockSpec((1, 1, D_MODEL), lambda b, i: (b, 0, 0))
    return pl.pallas_call(
        _ffn_kernel,
        grid=(nb, s // tm),
        in_specs=[tok, per_batch, per_batch, per_batch, _const_spec((1, D_MODEL)), _const_spec((1, D_MODEL)),
                  _const_spec((D_MODEL, 2 * D_FF)), _const_spec((3, 2 * D_FF)), _const_spec((1, 2 * D_FF)),
                  _const_spec((D_FF, D_MODEL))],
        out_specs=tok,
        out_shape=jax.ShapeDtypeStruct(x.shape, F32),
        scratch_shapes=[pltpu.VMEM((8, 2 * D_FF), F32)],
        compiler_params=pltpu.CompilerParams(
            dimension_semantics=("parallel", "arbitrary"), vmem_limit_bytes=VMEM_LIMIT),
        name="conv_ffn",
    )(x, sh, sc, g2, gpre, gpost, w1, cw, cb, w2)


def _reorder_w_in(w):
    def dup(block):
        heads = [block[:, j * HEAD_DIM:(j + 1) * HEAD_DIM] for j in range(N_KV_HEADS)]
        return jnp.concatenate([t for hd in heads for t in (hd, hd)], axis=1)

    k0 = 2 * GM_WIDTH + ATT_WIDTH
    kv = N_KV_HEADS * HEAD_DIM
    return jnp.concatenate(
        [w[:, :k0], dup(w[:, k0:k0 + kv]), dup(w[:, k0 + kv:k0 + 2 * kv]), w[:, k0 + 2 * kv:]], axis=1)


def kernel(x, c, positions, ada_w, ada_b, norm_mix_pre, norm_mix_post, norm_ffn_pre, norm_ffn_post, w_in,
           gm_ln_g, gm_ln_b, gm_ws, gm_bs, attn_sinks, hg_lb_logits, hg_gnorm, proj_a, proj_b, proj_c, w_out,
           ffn_w1, ffn_conv_w, ffn_conv_b, ffn_w2):
    nb = x.shape[0]
    mod = _ada_mod(c, ada_w, ada_b)
    lb_all = _forget_lower_bounds(hg_lb_logits)
    cos_t, sin_t = _rope_tables(positions)
    pfx = jnp.asarray(_HG_PFX_NP, dtype=BF16)
    lvl = jnp.asarray(_HG_LVL_NP)

    def row(v):
        return v.reshape(1, -1)

    for l in range(DEPTH):
        m = mod[l].reshape(nb, 1, 6, D_MODEL)
        sh1, sc1, g1, sh2, sc2, g2 = (m[:, :, t] for t in range(6))
        w = _reorder_w_in(w_in[l]).astype(BF16)
        uv, qkv, hg, lf, gates = _in_call(
            x, sh1, sc1, row(norm_mix_pre[l]), w, cos_t, sin_t, row(lb_all[l]),
            row(gm_ln_g[l]), row(gm_ln_b[l]))
        bs_b = jnp.broadcast_to(gm_bs[l][:, :, None], (GM_GROUPS, BLK, BLK))
        x = _mix_call(attn_sinks[l], x, uv, qkv, hg, lf, gates, g1, row(norm_mix_post[l]),
                      gm_ws[l], bs_b, pfx, lvl, row(hg_gnorm[l]),
                      proj_a[l].astype(BF16), proj_b[l].astype(BF16), proj_c[l].astype(BF16),
                      w_out[l].astype(BF16))
        x = _ffn_call(x, sh2, sc2, g2, row(norm_ffn_pre[l]), row(norm_ffn_post[l]),
                      ffn_w1[l].astype(BF16), ffn_conv_w[l], row(ffn_conv_b[l]), ffn_w2[l].astype(BF16))
    return x
```

```python
import jax
import jax.numpy as jnp
import numpy as np
from jax import lax
from jax.experimental import pallas as pl
from jax.experimental.pallas import tpu as pltpu

F32 = jnp.float32
BF16 = jnp.bfloat16

D_MODEL = 1024
DEPTH = 4
GM_WIDTH = 512
GM_GROUPS = 4
N_Q_HEADS = 8
N_KV_HEADS = 2
HEAD_DIM = 64
ATT_WIDTH = N_Q_HEADS * HEAD_DIM
ROPE_THETA = 10000.0
MASK_VALUE = -1e30
HG_HEADS = 4
HG_EXPAND = 128
HG_KEY_WIDTH = HG_HEADS * HG_EXPAND
HG_WIDTH = HG_HEADS * 128
HG_MIN_FORGET = 1e-6
D_FF = 2816
EPS = 1e-6

LANES = 128
BLK = 128
MIX_SUB = 2
TM_IN = 256
TM_FFN = 256
FF_CHUNK = 256
VMEM_LIMIT = 56 * 1024 * 1024

C_GU, C_GV, C_Q, C_K, C_V, C_HQ, C_HF, C_HI, C_HG, C_GATE, C_END = (
    0, 512, 1024, 1536, 1792, 2048, 2560, 3072, 3584, 4096, 7168)

HG_LEVEL_HALVES = (64, 32, 16, 8)
HG_DIAG = 8


def _hgrn_tables():
    t = np.arange(BLK)
    tri = (t[None, :] <= t[:, None]).astype(np.float32)
    lvl = np.zeros((BLK, BLK), np.int32)
    for li, h in enumerate(HG_LEVEL_HALVES):
        blk = 2 * h
        second = (t % blk) >= h
        same = (t[:, None] // blk) == (t[None, :] // blk)
        lvl[same & second[:, None] & (~second)[None, :]] = li + 1
    same = (t[:, None] // HG_DIAG) == (t[None, :] // HG_DIAG)
    lvl[same & (t[None, :] <= t[:, None])] = len(HG_LEVEL_HALVES) + 1
    return tri, lvl


_HG_TRI_NP, _HG_LVL_NP = _hgrn_tables()


def _dot(a, b):
    return jnp.dot(a, b, preferred_element_type=F32)


def _dot_nt(a, b):
    return lax.dot_general(a, b, (((1,), (1,)), ((), ())), preferred_element_type=F32)


def _sigmoid(x):
    return 1.0 / (1.0 + jnp.exp(-x))


def _silu(x):
    return x * _sigmoid(x)


def _gelu_tanh(x):
    return 0.5 * x * (1.0 + jnp.tanh(np.sqrt(2.0 / np.pi).astype(np.float32) * (x + 0.044715 * (x * x * x))))


def _rms(x):
    return x * lax.rsqrt(jnp.mean(x * x, axis=-1, keepdims=True) + EPS)


def _const_spec(shape):
    nd = len(shape)
    return pl.BlockSpec(shape, lambda b, i: (0,) * nd, pipeline_mode=pl.Buffered(1))


def _mxu_weight(w):
    w = w.astype(BF16)
    if (w.shape[-1] // LANES) % 2 == 0:
        w = jnp.pad(w, ((0, 0), (0, LANES)))
    return w


def _ada_kernel(c_ref, w_ref, b_ref, o_ref):
    c = c_ref[...]
    o_ref[0] = jnp.dot(_silu(c), w_ref[0], preferred_element_type=F32,
                       precision=lax.Precision.HIGHEST) + b_ref[0]


def _ada_mod(c, ada_w, ada_b):
    nb = c.shape[0]
    ncol = ada_w.shape[2] // D_MODEL
    return pl.pallas_call(
        _ada_kernel,
        grid=(DEPTH, ncol),
        in_specs=[pl.BlockSpec((nb, D_MODEL), lambda l, j: (0, 0)),
                  pl.BlockSpec((1, D_MODEL, D_MODEL), lambda l, j: (l, 0, j)),
                  pl.BlockSpec((1, 1, D_MODEL), lambda l, j: (l, 0, j))],
        out_specs=pl.BlockSpec((1, nb, D_MODEL), lambda l, j: (l, 0, j)),
        out_shape=jax.ShapeDtypeStruct((DEPTH, nb, ncol * D_MODEL), F32),
        name="ada_mod",
    )(c, ada_w, ada_b.reshape(DEPTH, 1, -1))


def _lb_kernel(x_ref, o_ref):
    x = x_ref[...]
    e = jnp.exp(x - jnp.max(x, axis=0, keepdims=True))
    p = e / jnp.sum(e, axis=0, keepdims=True)
    run = jnp.zeros_like(p[0:1])
    rows = []
    for l in range(DEPTH):
        run = run + p[l:l + 1]
        rows.append(run - p[0:1])
    o_ref[...] = jnp.concatenate(rows, axis=0)


def _forget_lower_bounds(logits):
    return pl.pallas_call(
        _lb_kernel, out_shape=jax.ShapeDtypeStruct(logits.shape, F32), name="hgrn_lower_bounds",
    )(logits.astype(F32))


def _rope_kernel(pos_ref, inv_ref, sign_ref, cos_ref, sin_ref):
    ang = pos_ref[0] * inv_ref[...]
    cos_ref[0] = jnp.cos(ang)
    sin_ref[0] = jnp.sin(ang) * sign_ref[...]


def _rope_tables(positions):
    nb, s = positions.shape
    half = HEAD_DIM // 2
    inv = ROPE_THETA ** (-jnp.arange(half, dtype=F32) / half)
    inv_t = jnp.tile(inv, LANES // half)[None, :]
    sign = np.where((np.arange(LANES) % HEAD_DIM) < half, -1.0, 1.0).astype(np.float32)[None, :]
    ts = 512
    tab = jax.ShapeDtypeStruct((nb, s, LANES), F32)
    return pl.pallas_call(
        _rope_kernel,
        grid=(nb, s // ts),
        in_specs=[pl.BlockSpec((1, ts, 1), lambda b, i: (b, i, 0)),
                  pl.BlockSpec((1, LANES), lambda b, i: (0, 0)),
                  pl.BlockSpec((1, LANES), lambda b, i: (0, 0))],
        out_specs=[pl.BlockSpec((1, ts, LANES), lambda b, i: (b, i, 0))] * 2,
        out_shape=[tab, tab],
        name="rope_tables",
    )(positions.astype(F32)[..., None], inv_t, jnp.asarray(sign))


def _rope_apply(x, cos, sin_signed, first_half):
    partner = jnp.where(first_half, pltpu.roll(x, LANES - HEAD_DIM // 2, 1), pltpu.roll(x, HEAD_DIM // 2, 1))
    return x * cos + partner * sin_signed


def _in_kernel(x_ref, sh_ref, sc_ref, g_ref, w_ref, cos_ref, sin_ref, lb_ref, lng_ref, lnb_ref,
               uv_ref, qkv_ref, hg_ref, lf_ref, gate_ref):
    x = x_ref[0]
    h = _rms(x) * g_ref[...]
    h = h * (1.0 + sc_ref[0]) + sh_ref[0]
    hb = h.astype(BF16)

    def proj(a, b):
        return _dot(hb, w_ref[:, a:b])

    uv_ref[0, :, 0:GM_WIDTH] = _gelu_tanh(proj(C_GU, C_GV)).astype(BF16)
    v = _gelu_tanh(proj(C_GV, C_Q))
    mu = jnp.mean(v, axis=-1, keepdims=True)
    vc = v - mu
    var = jnp.mean(vc * vc, axis=-1, keepdims=True)
    vn = vc * lax.rsqrt(var + EPS) * lng_ref[...] + lnb_ref[...]
    uv_ref[0, :, GM_WIDTH:2 * GM_WIDTH] = vn.astype(BF16)

    cos = cos_ref[0]
    sin = sin_ref[0]
    lane = lax.broadcasted_iota(jnp.int32, cos.shape, 1)
    first_half = (lane % HEAD_DIM) < (HEAD_DIM // 2)
    qk = proj(C_Q, C_V)
    scale = HEAD_DIM ** -0.5
    for gidx in range((C_V - C_Q) // LANES):
        xg = qk[:, gidx * LANES:(gidx + 1) * LANES]
        r = _rope_apply(xg, cos, sin, first_half)
        if gidx < ATT_WIDTH // LANES:
            r = r * scale
        qkv_ref[0, :, gidx * LANES:(gidx + 1) * LANES] = r.astype(BF16)
    qkv_ref[0, :, C_V - C_Q:C_HQ - C_Q] = proj(C_V, C_HQ).astype(BF16)

    hg_ref[0, :, 0:512] = (_silu(proj(C_HQ, C_HF)) * (HG_EXPAND ** -0.5)).astype(BF16)
    fl = proj(C_HF, C_HI)
    lb = lb_ref[...]
    forget = lb + (1.0 - lb) * _sigmoid(fl)
    lf_ref[0] = jnp.log(jnp.maximum(forget, HG_MIN_FORGET))
    hg_ref[0, :, 512:1024] = ((1.0 - lb) * _sigmoid(-fl)).astype(BF16)
    hg_ref[0, :, 1024:1536] = proj(C_HI, C_HG).astype(BF16)
    hg_ref[0, :, 1536:2048] = _silu(proj(C_HG, C_GATE)).astype(BF16)

    for gidx in range(3 * D_MODEL // 512):
        a = C_GATE + gidx * 512
        gate_ref[0, :, gidx * 512:(gidx + 1) * 512] = _sigmoid(proj(a, a + 512)).astype(BF16)


def _in_call(x, sh, sc, g, w, cos_t, sin_t, lb, lng, lnb):
    nb, s, _ = x.shape
    tm = TM_IN

    def tok(width):
        return pl.BlockSpec((1, tm, width), lambda b, i: (b, i, 0))

    def per_batch(width):
        return pl.BlockSpec((1, 1, width), lambda b, i: (b, 0, 0))

    def out(width, dt):
        return jax.ShapeDtypeStruct((nb, s, width), dt)

    return pl.pallas_call(
        _in_kernel,
        grid=(nb, s // tm),
        in_specs=[tok(D_MODEL), per_batch(D_MODEL), per_batch(D_MODEL), _const_spec((1, D_MODEL)),
                  _const_spec(w.shape), tok(LANES), tok(LANES),
                  _const_spec((1, HG_KEY_WIDTH)), _const_spec((1, GM_WIDTH)), _const_spec((1, GM_WIDTH))],
        out_specs=[tok(1024), tok(1024), tok(2048), tok(512), tok(3072)],
        out_shape=[out(1024, BF16), out(1024, BF16), out(2048, BF16), out(512, F32), out(3072, BF16)],
        compiler_params=pltpu.CompilerParams(
            dimension_semantics=("parallel", "parallel"), vmem_limit_bytes=VMEM_LIMIT),
        name="mixer_in",
    )(x, sh, sc, g, w, cos_t, sin_t, lb, lng, lnb)


def _spatial_gating(uv, ws_bf, bs_ref):
    outs = []
    for g in range(GM_GROUPS):
        sl = slice(g * LANES, (g + 1) * LANES)
        s = _dot(ws_bf[g], uv[:, GM_WIDTH + g * LANES:GM_WIDTH + (g + 1) * LANES]) + bs_ref[g]
        outs.append(uv[:, sl].astype(F32) * s)
    return jnp.concatenate(outs, axis=1)


def _window_attention(qkv, k_prev, v_prev, sink_ref, first_key):
    qi = lax.broadcasted_iota(jnp.int32, (BLK, 2 * BLK), 0)
    kj = lax.broadcasted_iota(jnp.int32, (BLK, 2 * BLK), 1)
    valid = (kj > qi) & (kj <= qi + BLK) & (kj >= first_key)
    lane = lax.broadcasted_iota(jnp.int32, (2 * BLK, LANES), 1)
    low = lane < HEAD_DIM
    lane_o = lax.broadcasted_iota(jnp.int32, (BLK, LANES), 1)
    low_o = lane_o < HEAD_DIM
    pairs_per_kv = (N_Q_HEADS // N_KV_HEADS) // 2
    outs, k_new, v_new = [], [], []
    for j in range(N_KV_HEADS):
        k_cur = qkv[:, ATT_WIDTH + j * LANES:ATT_WIDTH + (j + 1) * LANES]
        v_cur = qkv[:, ATT_WIDTH + (N_KV_HEADS + j) * LANES:ATT_WIDTH + (N_KV_HEADS + j + 1) * LANES]
        kd = jnp.concatenate([k_prev[j], k_cur], axis=0)
        vd = jnp.concatenate([v_prev[j], v_cur], axis=0)
        zero = jnp.zeros_like(kd)
        k_sel = (jnp.where(low, kd, zero), jnp.where(low, zero, kd))
        for pp in range(pairs_per_kv):
            pair = j * pairs_per_kv + pp
            qp = qkv[:, pair * LANES:(pair + 1) * LANES]
            halves = []
            for par in range(2):
                sink = sink_ref[2 * pair + par]
                s = jnp.where(valid, _dot_nt(qp, k_sel[par]), MASK_VALUE)
                m = jnp.maximum(jnp.max(s, axis=-1, keepdims=True), sink)
                p = jnp.exp(s - m)
                den = jnp.sum(p, axis=-1, keepdims=True) + jnp.exp(sink - m)
                halves.append(_dot(p.astype(BF16), vd) * (1.0 / den))
            outs.append(jnp.where(low_o, halves[0], halves[1]))
        k_new.append(k_cur)
        v_new.append(v_cur)
    return jnp.concatenate(outs, axis=1), k_new, v_new


def _level_exponents(b, h):
    pieces = []
    for k in range(BLK // (2 * h)):
        blk = b[k * 2 * h:(k + 1) * 2 * h]
        r = blk[h - 1:h]
        pieces.append(r - blk[:h])
        pieces.append(blk[h:] - r)
    return jnp.concatenate(pieces, axis=0)


def _hgrn2_head(q, k, v, gs, lfh, st, tri, lvl, gn):
    n_lv = len(HG_LEVEL_HALVES)
    hi = lfh.astype(BF16)
    lo = (lfh - hi.astype(F32)).astype(BF16)
    e2 = _dot(tri, jnp.concatenate([hi, lo], axis=1))
    b = e2[:, :LANES] + e2[:, LANES:]
    o = _dot_nt((q * jnp.exp(b)).astype(BF16), st.astype(BF16))
    att = jnp.zeros((BLK, BLK), F32)
    for li, h in enumerate(HG_LEVEL_HALVES):
        e = jnp.exp(_level_exponents(b, h))
        a = _dot_nt((q * e).astype(BF16), (k * e).astype(BF16))
        att = jnp.where(lvl == li + 1, a, att)
    b3 = b.reshape(BLK // HG_DIAG, HG_DIAG, LANES)
    xd = (b3 - b3[:, HG_DIAG // 2 - 1:HG_DIAG // 2, :]).reshape(BLK, LANES)
    a = _dot_nt((q * jnp.exp(xd)).astype(BF16), (k * jnp.exp(-xd)).astype(BF16))
    att = jnp.where(lvl == n_lv + 1, a, att)
    o = o + _dot(att.astype(BF16), v)
    b_last = b[BLK - 1:BLK]
    k_end = (k * jnp.exp(b_last - b)).astype(BF16)
    vt = v.astype(F32).T.astype(BF16)
    st_new = st * jnp.exp(b_last) + _dot(vt, k_end)
    return _rms(o) * gn * gs, st_new


def _mix_kernel(sink_ref, x_ref, uv_ref, qkv_ref, hg_ref, lf_ref, gate_ref, g1_ref, gpost_ref,
                ws_ref, bs_ref, tri_ref, lvl_ref, gn_ref, pa_ref, pb_ref, pc_ref, wo_ref,
                o_ref, kprev_ref, vprev_ref, state_ref):
    step = pl.program_id(1)

    @pl.when(step == 0)
    def _():
        kprev_ref[...] = jnp.zeros_like(kprev_ref)
        vprev_ref[...] = jnp.zeros_like(vprev_ref)
        state_ref[...] = jnp.zeros_like(state_ref)

    row = lax.broadcasted_iota(jnp.int32, (BLK, BLK), 0)
    col = lax.broadcasted_iota(jnp.int32, (BLK, BLK), 1)
    ws_bf = [jnp.where(col <= row, ws_ref[g], 0.0).astype(BF16) for g in range(GM_GROUPS)]
    tri = tri_ref[...]
    lvl = lvl_ref[...]
    gn = gn_ref[...]
    k_prev = [kprev_ref[j] for j in range(N_KV_HEADS)]
    v_prev = [vprev_ref[j] for j in range(N_KV_HEADS)]
    states = [state_ref[hd] for hd in range(HG_HEADS)]
    ya, yb, yc = [], [], []
    for sb in range(MIX_SUB):
        rows = slice(sb * BLK, (sb + 1) * BLK)
        ya.append(_spatial_gating(uv_ref[0, rows, :], ws_bf, bs_ref))
        first_key = jnp.where(step > 0, 0, BLK) if sb == 0 else 0
        o_b, k_prev, v_prev = _window_attention(qkv_ref[0, rows, :], k_prev, v_prev, sink_ref, first_key)
        yb.append(o_b)
        hg = hg_ref[0, rows, :]
        lf = lf_ref[0, rows, :]
        heads = []
        for hd in range(HG_HEADS):
            sl = slice(hd * LANES, (hd + 1) * LANES)
            o_c, states[hd] = _hgrn2_head(
                hg[:, sl].astype(F32), hg[:, 512 + hd * LANES:512 + (hd + 1) * LANES].astype(F32),
                hg[:, 1024 + hd * LANES:1024 + (hd + 1) * LANES],
                hg[:, 1536 + hd * LANES:1536 + (hd + 1) * LANES].astype(F32),
                lf[:, sl], states[hd], tri, lvl, gn)
            heads.append(o_c)
        yc.append(jnp.concatenate(heads, axis=1))
    for j in range(N_KV_HEADS):
        kprev_ref[j] = k_prev[j]
        vprev_ref[j] = v_prev[j]
    for hd in range(HG_HEADS):
        state_ref[hd] = states[hd]
    y_a = jnp.concatenate(ya, axis=0).astype(BF16)
    y_b = jnp.concatenate(yb, axis=0).astype(BF16)
    y_c = jnp.concatenate(yc, axis=0).astype(BF16)
    gates = gate_ref[0]
    merged = (gates[:, 0:D_MODEL].astype(F32) * _dot(y_a, pa_ref[:, 0:D_MODEL])
              + gates[:, D_MODEL:2 * D_MODEL].astype(F32) * _dot(y_b, pb_ref[:, 0:D_MODEL])
              + gates[:, 2 * D_MODEL:3 * D_MODEL].astype(F32) * _dot(y_c, pc_ref[:, 0:D_MODEL]))
    y = _dot(merged.astype(BF16), wo_ref[:, 0:D_MODEL])
    o_ref[0] = x_ref[0] + g1_ref[0] * (_rms(y) * gpost_ref[...])


def _mix_call(sinks, x, uv, qkv, hg, lf, gates, g1, gpost, ws, bs_b, tri, lvl, gn, pa, pb, pc, wo):
    nb, s, _ = x.shape
    tb = MIX_SUB * BLK

    def tok(width):
        return pl.BlockSpec((1, tb, width), lambda b, i: (b, i, 0))

    return pl.pallas_call(
        _mix_kernel,
        grid=(nb, s // tb),
        in_specs=[pl.BlockSpec(memory_space=pltpu.SMEM),
                  tok(D_MODEL), tok(1024), tok(1024), tok(2048), tok(512), tok(3072),
                  pl.BlockSpec((1, 1, D_MODEL), lambda b, i: (b, 0, 0)), _const_spec((1, D_MODEL)),
                  _const_spec((GM_GROUPS, BLK, BLK)), _const_spec((GM_GROUPS, BLK, BLK)),
                  _const_spec((BLK, BLK)), _const_spec((BLK, BLK)), _const_spec((1, LANES)),
                  _const_spec(pa.shape), _const_spec(pb.shape), _const_spec(pc.shape), _const_spec(wo.shape)],
        out_specs=tok(D_MODEL),
        out_shape=jax.ShapeDtypeStruct(x.shape, F32),
        scratch_shapes=[pltpu.VMEM((N_KV_HEADS, BLK, LANES), BF16),
                        pltpu.VMEM((N_KV_HEADS, BLK, LANES), BF16),
                        pltpu.VMEM((HG_HEADS, LANES, HG_EXPAND), F32)],
        compiler_params=pltpu.CompilerParams(
            dimension_semantics=("parallel", "arbitrary"), vmem_limit_bytes=VMEM_LIMIT),
        name="mixer_core",
    )(sinks, x, uv, qkv, hg, lf, gates, g1, gpost, ws, bs_b, tri, lvl, gn, pa, pb, pc, wo)


def _shift_rows(a, halo, n):
    t, c = a.shape
    r = pltpu.roll(a.reshape(t // 8, 8, c), n, axis=1)
    hr = pltpu.roll(halo.reshape(1, 8, c), n, axis=1)
    prev = jnp.concatenate([hr, r[:-1]], axis=0)
    row8 = lax.broadcasted_iota(jnp.int32, r.shape, 1)
    return jnp.where(row8 < n, prev, r).reshape(t, c)


def _ffn_kernel(x_ref, sh_ref, sc_ref, gate_ref, gpre_ref, gpost_ref, w1_ref, cw_ref, cb_ref, w2_ref,
                o_ref, halo_ref, hid_ref):
    step = pl.program_id(1)

    @pl.when(step == 0)
    def _():
        halo_ref[...] = jnp.zeros_like(halo_ref)

    x = x_ref[0]
    tm = x.shape[0]
    h = _rms(x) * gpre_ref[...]
    hb = (h * (1.0 + sc_ref[0]) + sh_ref[0]).astype(BF16)

    def conv(col):
        sl = slice(col, col + FF_CHUNK)
        a = _dot(hb, w1_ref[:, sl])
        halo = halo_ref[:, sl]
        a1 = _shift_rows(a, halo, 1)
        a2 = _shift_rows(a, halo, 2)
        halo_ref[:, sl] = a[tm - 8:tm]
        cw = cw_ref[:, sl]
        return cw[0:1] * a2 + cw[1:2] * a1 + cw[2:3] * a + cb_ref[:, sl]

    for c in range(D_FF // FF_CHUNK):
        gate = conv(c * FF_CHUNK)
        val = conv(D_FF + c * FF_CHUNK)
        hid_ref[:, c * FF_CHUNK:(c + 1) * FF_CHUNK] = (_silu(gate) * val).astype(BF16)
    y = _dot(hid_ref[...], w2_ref[:, 0:D_MODEL])
    o_ref[0] = x + gate_ref[0] * (_rms(y) * gpost_ref[...])


def _ffn_call(x, sh, sc, g2, gpre, gpost, w1, cw, cb, w2):
    nb, s, _ = x.shape
    tm = TM_FFN
    tok = pl.BlockSpec((1, tm, D_MODEL), lambda b, i: (b, i, 0))
    per_batch = pl.BlockSpec((1, 1, D_MODEL), lambda b, i: (b, 0, 0))
    return pl.pallas_call(
        _ffn_kernel,
        grid=(nb, s // tm),
        in_specs=[tok, per_batch, per_batch, per_batch, _const_spec((1, D_MODEL)), _const_spec((1, D_MODEL)),
                  _const_spec(w1.shape), _const_spec((3, 2 * D_FF)), _const_spec((1, 2 * D_FF)),
                  _const_spec(w2.shape)],
        out_specs=tok,
        out_shape=jax.ShapeDtypeStruct(x.shape, F32),
        scratch_shapes=[pltpu.VMEM((8, 2 * D_FF), F32), pltpu.VMEM((tm, D_FF), BF16)],
        compiler_params=pltpu.CompilerParams(
            dimension_semantics=("parallel", "arbitrary"), vmem_limit_bytes=VMEM_LIMIT),
        name="conv_ffn",
    )(x, sh, sc, g2, gpre, gpost, w1, cw, cb, w2)


def _reorder_w_in(w):
    def dup(block):
        heads = [block[:, j * HEAD_DIM:(j + 1) * HEAD_DIM] for j in range(N_KV_HEADS)]
        return jnp.concatenate([t for hd in heads for t in (hd, hd)], axis=1)

    k0 = 2 * GM_WIDTH + ATT_WIDTH
    kv = N_KV_HEADS * HEAD_DIM
    return jnp.concatenate(
        [w[:, :k0], dup(w[:, k0:k0 + kv]), dup(w[:, k0 + kv:k0 + 2 * kv]), w[:, k0 + 2 * kv:]], axis=1)


def kernel(x, c, positions, ada_w, ada_b, norm_mix_pre, norm_mix_post, norm_ffn_pre, norm_ffn_post, w_in,
           gm_ln_g, gm_ln_b, gm_ws, gm_bs, attn_sinks, hg_lb_logits, hg_gnorm, proj_a, proj_b, proj_c, w_out,
           ffn_w1, ffn_conv_w, ffn_conv_b, ffn_w2):
    nb = x.shape[0]
    mod = _ada_mod(c, ada_w, ada_b)
    lb_all = _forget_lower_bounds(hg_lb_logits)
    cos_t, sin_t = _rope_tables(positions)
    tri = jnp.asarray(_HG_TRI_NP, dtype=BF16)
    lvl = jnp.asarray(_HG_LVL_NP)

    def row(v):
        return v.reshape(1, -1)

    for l in range(DEPTH):
        m = mod[l].reshape(nb, 1, 6, D_MODEL)
        sh1, sc1, g1, sh2, sc2, g2 = (m[:, :, t] for t in range(6))
        uv, qkv, hg, lf, gates = _in_call(
            x, sh1, sc1, row(norm_mix_pre[l]), _mxu_weight(_reorder_w_in(w_in[l])), cos_t, sin_t,
            row(lb_all[l]), row(gm_ln_g[l]), row(gm_ln_b[l]))
        bs_b = jnp.broadcast_to(gm_bs[l][:, :, None], (GM_GROUPS, BLK, BLK))
        x = _mix_call(attn_sinks[l], x, uv, qkv, hg, lf, gates, g1, row(norm_mix_post[l]),
                      gm_ws[l], bs_b, tri, lvl, row(hg_gnorm[l]),
                      _mxu_weight(proj_a[l]), _mxu_weight(proj_b[l]), _mxu_weight(proj_c[l]),
                      _mxu_weight(w_out[l]))
        x = _ffn_call(x, sh2, sc2, g2, row(norm_ffn_pre[l]), row(norm_ffn_post[l]),
                      _mxu_weight(ffn_w1[l]), ffn_conv_w[l], row(ffn_conv_b[l]), _mxu_weight(ffn_w2[l]))
    return x
```

```python
import jax
import jax.numpy as jnp
import numpy as np
from jax import lax
from jax.experimental import pallas as pl
from jax.experimental.pallas import tpu as pltpu

F32 = jnp.float32
BF16 = jnp.bfloat16

D_MODEL = 1024
DEPTH = 4
GM_WIDTH = 512
GM_GROUPS = 4
N_Q_HEADS = 8
N_KV_HEADS = 2
HEAD_DIM = 64
ATT_WIDTH = N_Q_HEADS * HEAD_DIM
ROPE_THETA = 10000.0
MASK_VALUE = -1e30
HG_HEADS = 4
HG_EXPAND = 128
HG_KEY_WIDTH = HG_HEADS * HG_EXPAND
HG_WIDTH = HG_HEADS * 128
HG_MIN_FORGET = 1e-6
D_FF = 2816
EPS = 1e-6

LANES = 128
BLK = 128
MIX_SUB = 2
TM_IN = 256
TM_FFN = 512
FF_CHUNK = 256
VMEM_LIMIT = 56 * 1024 * 1024

C_GU, C_GV, C_Q, C_K, C_V, C_HQ, C_HF, C_HI, C_HG, C_GATE, C_END = (
    0, 512, 1024, 1536, 1792, 2048, 2560, 3072, 3584, 4096, 7168)

HG_LEVEL_HALVES = (64, 32, 16, 8)
HG_DIAG = 8


def _hgrn_tables():
    t = np.arange(BLK)
    tri = (t[None, :] <= t[:, None]).astype(np.float32)
    lvl = np.zeros((BLK, BLK), np.int32)
    for li, h in enumerate(HG_LEVEL_HALVES):
        blk = 2 * h
        second = (t % blk) >= h
        same = (t[:, None] // blk) == (t[None, :] // blk)
        lvl[same & second[:, None] & (~second)[None, :]] = li + 1
    same = (t[:, None] // HG_DIAG) == (t[None, :] // HG_DIAG)
    lvl[same & (t[None, :] <= t[:, None])] = len(HG_LEVEL_HALVES) + 1
    return tri, lvl


_HG_TRI_NP, _HG_LVL_NP = _hgrn_tables()


def _dot(a, b):
    return jnp.dot(a, b, preferred_element_type=F32)


def _dot_nt(a, b):
    return lax.dot_general(a, b, (((1,), (1,)), ((), ())), preferred_element_type=F32)


def _sigmoid(x):
    return 1.0 / (1.0 + jnp.exp(-x))


def _silu(x):
    return x * _sigmoid(x)


def _gelu_tanh(x):
    return 0.5 * x * (1.0 + jnp.tanh(np.sqrt(2.0 / np.pi).astype(np.float32) * (x + 0.044715 * (x * x * x))))


def _rms(x):
    return x * lax.rsqrt(jnp.mean(x * x, axis=-1, keepdims=True) + EPS)


def _const_spec(shape):
    nd = len(shape)
    return pl.BlockSpec(shape, lambda b, i: (0,) * nd, pipeline_mode=pl.Buffered(1))


def _mxu_weight(w):
    w = w.astype(BF16)
    if (w.shape[-1] // LANES) % 2 == 0:
        w = jnp.pad(w, ((0, 0), (0, LANES)))
    return w


def _ada_kernel(c_ref, w_ref, b_ref, o_ref):
    c = c_ref[...]
    o_ref[0] = jnp.dot(_silu(c), w_ref[0], preferred_element_type=F32,
                       precision=lax.Precision.HIGHEST) + b_ref[0]


def _ada_mod(c, ada_w, ada_b):
    nb = c.shape[0]
    ncol = ada_w.shape[2] // D_MODEL
    return pl.pallas_call(
        _ada_kernel,
        grid=(DEPTH, ncol),
        in_specs=[pl.BlockSpec((nb, D_MODEL), lambda l, j: (0, 0)),
                  pl.BlockSpec((1, D_MODEL, D_MODEL), lambda l, j: (l, 0, j)),
                  pl.BlockSpec((1, 1, D_MODEL), lambda l, j: (l, 0, j))],
        out_specs=pl.BlockSpec((1, nb, D_MODEL), lambda l, j: (l, 0, j)),
        out_shape=jax.ShapeDtypeStruct((DEPTH, nb, ncol * D_MODEL), F32),
        name="ada_mod",
    )(c, ada_w, ada_b.reshape(DEPTH, 1, -1))


def _lb_kernel(x_ref, o_ref):
    x = x_ref[...]
    e = jnp.exp(x - jnp.max(x, axis=0, keepdims=True))
    p = e / jnp.sum(e, axis=0, keepdims=True)
    run = jnp.zeros_like(p[0:1])
    rows = []
    for l in range(DEPTH):
        run = run + p[l:l + 1]
        rows.append(run - p[0:1])
    o_ref[...] = jnp.concatenate(rows, axis=0)


def _forget_lower_bounds(logits):
    return pl.pallas_call(
        _lb_kernel, out_shape=jax.ShapeDtypeStruct(logits.shape, F32), name="hgrn_lower_bounds",
    )(logits.astype(F32))


def _rope_kernel(pos_ref, inv_ref, sign_ref, cos_ref, sin_ref):
    ang = pos_ref[0] * inv_ref[...]
    cos_ref[0] = jnp.cos(ang)
    sin_ref[0] = jnp.sin(ang) * sign_ref[...]


def _rope_tables(positions):
    nb, s = positions.shape
    half = HEAD_DIM // 2
    inv = ROPE_THETA ** (-jnp.arange(half, dtype=F32) / half)
    inv_t = jnp.tile(inv, LANES // half)[None, :]
    sign = np.where((np.arange(LANES) % HEAD_DIM) < half, -1.0, 1.0).astype(np.float32)[None, :]
    ts = 512
    tab = jax.ShapeDtypeStruct((nb, s, LANES), F32)
    return pl.pallas_call(
        _rope_kernel,
        grid=(nb, s // ts),
        in_specs=[pl.BlockSpec((1, ts, 1), lambda b, i: (b, i, 0)),
                  pl.BlockSpec((1, LANES), lambda b, i: (0, 0)),
                  pl.BlockSpec((1, LANES), lambda b, i: (0, 0))],
        out_specs=[pl.BlockSpec((1, ts, LANES), lambda b, i: (b, i, 0))] * 2,
        out_shape=[tab, tab],
        name="rope_tables",
    )(positions.astype(F32)[..., None], inv_t, jnp.asarray(sign))


def _rope_apply(x, cos, sin_signed, first_half):
    partner = jnp.where(first_half, pltpu.roll(x, LANES - HEAD_DIM // 2, 1), pltpu.roll(x, HEAD_DIM // 2, 1))
    return x * cos + partner * sin_signed


def _in_kernel(x_ref, sh_ref, sc_ref, g_ref, w_ref, cos_ref, sin_ref, lb_ref, lng_ref, lnb_ref,
               uv_ref, qkv_ref, hg_ref, lf_ref, gate_ref):
    x = x_ref[0]
    h = _rms(x) * g_ref[...]
    h = h * (1.0 + sc_ref[0]) + sh_ref[0]
    hb = h.astype(BF16)

    def proj(a, b):
        return _dot(hb, w_ref[:, a:b])

    cos = cos_ref[0]
    sin = sin_ref[0]
    lane = lax.broadcasted_iota(jnp.int32, cos.shape, 1)
    first_half = (lane % HEAD_DIM) < (HEAD_DIM // 2)
    lb = lb_ref[...]

    def epi_u(d):
        uv_ref[0, :, 0:GM_WIDTH] = _gelu_tanh(d).astype(BF16)

    def epi_v(d):
        v = _gelu_tanh(d)
        mu = jnp.mean(v, axis=-1, keepdims=True)
        vc = v - mu
        var = jnp.mean(vc * vc, axis=-1, keepdims=True)
        uv_ref[0, :, GM_WIDTH:2 * GM_WIDTH] = (vc * lax.rsqrt(var + EPS) * lng_ref[...] + lnb_ref[...]).astype(BF16)

    def epi_rope(base, scale):
        def f(d):
            for gidx in range(d.shape[1] // LANES):
                r = _rope_apply(d[:, gidx * LANES:(gidx + 1) * LANES], cos, sin, first_half)
                if scale != 1.0:
                    r = r * scale
                qkv_ref[0, :, base + gidx * LANES:base + (gidx + 1) * LANES] = r.astype(BF16)
        return f

    def epi_vdup(d):
        qkv_ref[0, :, C_V - C_Q:C_HQ - C_Q] = d.astype(BF16)

    def epi_hq(d):
        hg_ref[0, :, 0:512] = (_silu(d) * (HG_EXPAND ** -0.5)).astype(BF16)

    def epi_hf(fl):
        forget = lb + (1.0 - lb) * _sigmoid(fl)
        lf_ref[0] = jnp.log(jnp.maximum(forget, HG_MIN_FORGET))
        hg_ref[0, :, 512:1024] = ((1.0 - lb) * _sigmoid(-fl)).astype(BF16)

    def epi_hi(d):
        hg_ref[0, :, 1024:1536] = d.astype(BF16)

    def epi_hgate(d):
        hg_ref[0, :, 1536:2048] = _silu(d).astype(BF16)

    def epi_gate(gidx):
        def f(d):
            gate_ref[0, :, gidx * 512:(gidx + 1) * 512] = _sigmoid(d).astype(BF16)
        return f

    stages = [(C_GU, C_GV, epi_u), (C_GV, C_Q, epi_v), (C_Q, C_K, epi_rope(0, HEAD_DIM ** -0.5)),
              (C_K, C_V, epi_rope(C_K - C_Q, 1.0)), (C_V, C_HQ, epi_vdup), (C_HQ, C_HF, epi_hq),
              (C_HF, C_HI, epi_hf), (C_HG, C_GATE, epi_hgate)]
    stages += [(C_GATE + g * 512, C_GATE + (g + 1) * 512, epi_gate(g)) for g in range(3 * D_MODEL // 512)]
    stages += [(C_HI, C_HG, epi_hi)]
    for a, b, epi in stages:
        epi(proj(a, b))


def _in_call(x, sh, sc, g, w, cos_t, sin_t, lb, lng, lnb):
    nb, s, _ = x.shape
    tm = TM_IN

    def tok(width):
        return pl.BlockSpec((1, tm, width), lambda b, i: (b, i, 0))

    def per_batch(width):
        return pl.BlockSpec((1, 1, width), lambda b, i: (b, 0, 0))

    def out(width, dt):
        return jax.ShapeDtypeStruct((nb, s, width), dt)

    return pl.pallas_call(
        _in_kernel,
        grid=(nb, s // tm),
        in_specs=[tok(D_MODEL), per_batch(D_MODEL), per_batch(D_MODEL), _const_spec((1, D_MODEL)),
                  _const_spec(w.shape), tok(LANES), tok(LANES),
                  _const_spec((1, HG_KEY_WIDTH)), _const_spec((1, GM_WIDTH)), _const_spec((1, GM_WIDTH))],
        out_specs=[tok(1024), tok(1024), tok(2048), tok(512), tok(3072)],
        out_shape=[out(1024, BF16), out(1024, BF16), out(2048, BF16), out(512, F32), out(3072, BF16)],
        compiler_params=pltpu.CompilerParams(
            dimension_semantics=("parallel", "parallel"), vmem_limit_bytes=VMEM_LIMIT),
        name="mixer_in",
    )(x, sh, sc, g, w, cos_t, sin_t, lb, lng, lnb)


def _spatial_gating(uv, ws_bf, bs_ref):
    outs = []
    for g in range(GM_GROUPS):
        sl = slice(g * LANES, (g + 1) * LANES)
        s = _dot(ws_bf[g], uv[:, GM_WIDTH + g * LANES:GM_WIDTH + (g + 1) * LANES]) + bs_ref[g]
        outs.append(uv[:, sl].astype(F32) * s)
    return jnp.concatenate(outs, axis=1)


def _window_attention(qkv, k_prev, v_prev, sink_ref, first_key):
    nk = 2 * BLK
    qi = lax.broadcasted_iota(jnp.int32, (BLK, nk), 0)
    kj = lax.broadcasted_iota(jnp.int32, (BLK, nk), 1)
    valid = (kj > qi) & (kj <= qi + BLK) & (kj >= first_key)
    row_t = lax.broadcasted_iota(jnp.int32, (LANES, nk), 0)
    low_t = row_t < HEAD_DIM
    lane_o = lax.broadcasted_iota(jnp.int32, (BLK, LANES), 1)
    low_o = lane_o < HEAD_DIM
    pairs_per_kv = (N_Q_HEADS // N_KV_HEADS) // 2
    outs, k_new, v_new = [], [], []
    for j in range(N_KV_HEADS):
        k_cur = qkv[:, ATT_WIDTH + j * LANES:ATT_WIDTH + (j + 1) * LANES]
        v_cur = qkv[:, ATT_WIDTH + (N_KV_HEADS + j) * LANES:ATT_WIDTH + (N_KV_HEADS + j + 1) * LANES]
        kt = jnp.concatenate([k_prev[j], k_cur], axis=0).astype(F32).T
        zero = jnp.zeros_like(kt)
        k_sel = jnp.concatenate([jnp.where(low_t, kt, zero), jnp.where(low_t, zero, kt)], axis=1).astype(BF16)
        vd = jnp.concatenate([v_prev[j], v_cur], axis=0)
        q4 = jnp.concatenate([qkv[:, (j * pairs_per_kv + pp) * LANES:(j * pairs_per_kv + pp + 1) * LANES]
                              for pp in range(pairs_per_kv)], axis=0)
        s_all = _dot(q4, k_sel)
        probs, dens = [], []
        for pp in range(pairs_per_kv):
            for par in range(2):
                sink = sink_ref[2 * (j * pairs_per_kv + pp) + par]
                s = jnp.where(valid, s_all[pp * BLK:(pp + 1) * BLK, par * nk:(par + 1) * nk], MASK_VALUE)
                m = jnp.maximum(jnp.max(s, axis=-1, keepdims=True), sink)
                p = jnp.exp(s - m)
                dens.append(jnp.sum(p, axis=-1, keepdims=True) + jnp.exp(sink - m))
                probs.append(p.astype(BF16))
        o_all = _dot(jnp.concatenate(probs, axis=0), vd)
        for pp in range(pairs_per_kv):
            halves = [o_all[(2 * pp + par) * BLK:(2 * pp + par + 1) * BLK] * (1.0 / dens[2 * pp + par])
                      for par in range(2)]
            outs.append(jnp.where(low_o, halves[0], halves[1]))
        k_new.append(k_cur)
        v_new.append(v_cur)
    return jnp.concatenate(outs, axis=1), k_new, v_new


def _level_exponents(b, h):
    pieces = []
    for k in range(BLK // (2 * h)):
        blk = b[k * 2 * h:(k + 1) * 2 * h]
        r = blk[h - 1:h]
        pieces.append(r - blk[:h])
        pieces.append(blk[h:] - r)
    return jnp.concatenate(pieces, axis=0)


def _block_diag(x2, top_mask):
    zero = jnp.zeros_like(x2)
    return jnp.concatenate([jnp.where(top_mask, x2, zero), jnp.where(top_mask, zero, x2)], axis=0)


def _block_diag_t(x2):
    xt = x2.T
    row = lax.broadcasted_iota(jnp.int32, xt.shape, 0)
    top = row < LANES
    zero = jnp.zeros_like(xt)
    return jnp.concatenate([jnp.where(top, xt, zero), jnp.where(top, zero, xt)], axis=1).astype(BF16)


def _hgrn2_pair(q, k, v, gs, lf, st, tri2, lvl2, gn2):
    n_lv = len(HG_LEVEL_HALVES)
    w2 = 2 * LANES
    lane = lax.broadcasted_iota(jnp.int32, (BLK, w2), 1)
    first = lane < LANES
    lane_s = lax.broadcasted_iota(jnp.int32, (w2, w2), 1)
    row_s = lax.broadcasted_iota(jnp.int32, (w2, w2), 0)
    diag_blocks = (lane_s < LANES) == (row_s < LANES)
    hi = lf.astype(BF16)
    lo = (lf - hi.astype(F32)).astype(BF16)
    b = _dot(tri2, jnp.concatenate([hi, lo], axis=0))
    o = _dot_nt((q * jnp.exp(b)).astype(BF16), st.astype(BF16))
    att = jnp.zeros((BLK, w2), F32)
    for li, h in enumerate(HG_LEVEL_HALVES):
        e = jnp.exp(_level_exponents(b, h))
        a = _dot((q * e).astype(BF16), _block_diag_t(k * e))
        att = jnp.where(lvl2 == li + 1, a, att)
    b3 = b.reshape(BLK // HG_DIAG, HG_DIAG, w2)
    xd = (b3 - b3[:, HG_DIAG // 2 - 1:HG_DIAG // 2, :]).reshape(BLK, w2)
    a = _dot((q * jnp.exp(xd)).astype(BF16), _block_diag_t(k * jnp.exp(-xd)))
    att = jnp.where(lvl2 == n_lv + 1, a, att)
    o = o + _dot(att.astype(BF16), _block_diag(v, first))
    b_last = b[BLK - 1:BLK]
    k_end = (k * jnp.exp(b_last - b)).astype(BF16)
    vt = v.astype(F32).T.astype(BF16)
    st_new = jnp.where(diag_blocks, st * jnp.exp(b_last) + _dot(vt, k_end), 0.0)
    outs = [_rms(o[:, t * LANES:(t + 1) * LANES]) for t in range(2)]
    return jnp.concatenate(outs, axis=1) * gn2 * gs, st_new


def _mix_kernel(sink_ref, x_ref, uv_ref, qkv_ref, hg_ref, lf_ref, gate_ref, g1_ref, gpost_ref,
                ws_ref, bs_ref, tri_ref, lvl_ref, gn_ref, pa_ref, pb_ref, pc_ref, wo_ref,
                o_ref, kprev_ref, vprev_ref, state_ref):
    step = pl.program_id(1)

    @pl.when(step == 0)
    def _():
        kprev_ref[...] = jnp.zeros_like(kprev_ref)
        vprev_ref[...] = jnp.zeros_like(vprev_ref)
        state_ref[...] = jnp.zeros_like(state_ref)

    row = lax.broadcasted_iota(jnp.int32, (BLK, BLK), 0)
    col = lax.broadcasted_iota(jnp.int32, (BLK, BLK), 1)
    ws_bf = [jnp.where(col <= row, ws_ref[g], 0.0).astype(BF16) for g in range(GM_GROUPS)]
    tri2 = tri_ref[...]
    lvl2 = lvl_ref[...]
    gn2 = gn_ref[...]
    k_prev = [kprev_ref[j] for j in range(N_KV_HEADS)]
    v_prev = [vprev_ref[j] for j in range(N_KV_HEADS)]
    states = [state_ref[p] for p in range(HG_HEADS // 2)]
    ya, yb, yc = [], [], []
    for sb in range(MIX_SUB):
        rows = slice(sb * BLK, (sb + 1) * BLK)
        ya.append(_spatial_gating(uv_ref[0, rows, :], ws_bf, bs_ref))
        first_key = jnp.where(step > 0, 0, BLK) if sb == 0 else 0
        o_b, k_prev, v_prev = _window_attention(qkv_ref[0, rows, :], k_prev, v_prev, sink_ref, first_key)
        yb.append(o_b)
        hg = hg_ref[0, rows, :]
        lf = lf_ref[0, rows, :]
        heads = []
        for p in range(HG_HEADS // 2):
            def part(base):
                return hg[:, base + p * 2 * LANES:base + (p + 1) * 2 * LANES]
            o_c, states[p] = _hgrn2_pair(
                part(0).astype(F32), part(512).astype(F32), part(1024), part(1536).astype(F32),
                lf[:, p * 2 * LANES:(p + 1) * 2 * LANES], states[p], tri2, lvl2, gn2)
            heads.append(o_c)
        yc.append(jnp.concatenate(heads, axis=1))
    for j in range(N_KV_HEADS):
        kprev_ref[j] = k_prev[j]
        vprev_ref[j] = v_prev[j]
    for p in range(HG_HEADS // 2):
        state_ref[p] = states[p]
    y_a = jnp.concatenate(ya, axis=0).astype(BF16)
    y_b = jnp.concatenate(yb, axis=0).astype(BF16)
    y_c = jnp.concatenate(yc, axis=0).astype(BF16)
    gates = gate_ref[0]
    merged = (gates[:, 0:D_MODEL].astype(F32) * _dot(y_a, pa_ref[:, 0:D_MODEL])
              + gates[:, D_MODEL:2 * D_MODEL].astype(F32) * _dot(y_b, pb_ref[:, 0:D_MODEL])
              + gates[:, 2 * D_MODEL:3 * D_MODEL].astype(F32) * _dot(y_c, pc_ref[:, 0:D_MODEL]))
    y = _dot(merged.astype(BF16), wo_ref[:, 0:D_MODEL])
    o_ref[0] = x_ref[0] + g1_ref[0] * (_rms(y) * gpost_ref[...])


def _mix_call(sinks, x, uv, qkv, hg, lf, gates, g1, gpost, ws, bs_b, tri, lvl, gn, pa, pb, pc, wo):
    nb, s, _ = x.shape
    tb = MIX_SUB * BLK

    def tok(width):
        return pl.BlockSpec((1, tb, width), lambda b, i: (b, i, 0))

    return pl.pallas_call(
        _mix_kernel,
        grid=(nb, s // tb),
        in_specs=[pl.BlockSpec(memory_space=pltpu.SMEM),
                  tok(D_MODEL), tok(1024), tok(1024), tok(2048), tok(512), tok(3072),
                  pl.BlockSpec((1, 1, D_MODEL), lambda b, i: (b, 0, 0)), _const_spec((1, D_MODEL)),
                  _const_spec((GM_GROUPS, BLK, BLK)), _const_spec((GM_GROUPS, BLK, BLK)),
                  _const_spec((BLK, 2 * BLK)), _const_spec((BLK, 2 * BLK)), _const_spec((1, 2 * LANES)),
                  _const_spec(pa.shape), _const_spec(pb.shape), _const_spec(pc.shape), _const_spec(wo.shape)],
        out_specs=tok(D_MODEL),
        out_shape=jax.ShapeDtypeStruct(x.shape, F32),
        scratch_shapes=[pltpu.VMEM((N_KV_HEADS, BLK, LANES), BF16),
                        pltpu.VMEM((N_KV_HEADS, BLK, LANES), BF16),
                        pltpu.VMEM((HG_HEADS // 2, 2 * LANES, 2 * HG_EXPAND), F32)],
        compiler_params=pltpu.CompilerParams(
            dimension_semantics=("parallel", "arbitrary"), vmem_limit_bytes=VMEM_LIMIT),
        name="mixer_core",
    )(sinks, x, uv, qkv, hg, lf, gates, g1, gpost, ws, bs_b, tri, lvl, gn, pa, pb, pc, wo)


def _ffn_kernel(x_ref, sh_ref, sc_ref, gate_ref, gpre_ref, gpost_ref, w1_ref, cw_ref, cb_ref, w2_ref,
                o_ref, a_ref, hid_ref):
    step = pl.program_id(1)
    tm = x_ref.shape[1]

    @pl.when(step == 0)
    def _():
        a_ref[:, 0:8, :] = jnp.zeros((a_ref.shape[0], 8, LANES), F32)

    x = x_ref[0]
    h = _rms(x) * gpre_ref[...]
    hb = (h * (1.0 + sc_ref[0]) + sh_ref[0]).astype(BF16)

    def conv(col):
        a = _dot(hb, w1_ref[:, col:col + FF_CHUNK])
        taps = []
        for t in range(FF_CHUNK // LANES):
            a_ref[col // LANES + t, 8:8 + tm, :] = a[:, t * LANES:(t + 1) * LANES]
        for t in range(FF_CHUNK // LANES):
            slab = col // LANES + t
            sl = slice(col + t * LANES, col + (t + 1) * LANES)
            cw = cw_ref[:, sl]
            taps.append(cw[0:1] * a_ref[slab, 6:6 + tm, :] + cw[1:2] * a_ref[slab, 7:7 + tm, :]
                        + cw[2:3] * a[:, t * LANES:(t + 1) * LANES] + cb_ref[:, sl])
            a_ref[slab, 0:8, :] = a[tm - 8:tm, t * LANES:(t + 1) * LANES]
        return jnp.concatenate(taps, axis=1)

    for c in range(D_FF // FF_CHUNK):
        gate = conv(c * FF_CHUNK)
        val = conv(D_FF + c * FF_CHUNK)
        hid_ref[:, c * FF_CHUNK:(c + 1) * FF_CHUNK] = (_silu(gate) * val).astype(BF16)
    y = _dot(hid_ref[...], w2_ref[:, 0:D_MODEL])
    o_ref[0] = x + gate_ref[0] * (_rms(y) * gpost_ref[...])


def _ffn_call(x, sh, sc, g2, gpre, gpost, w1, cw, cb, w2):
    nb, s, _ = x.shape
    tm = TM_FFN
    tok = pl.BlockSpec((1, tm, D_MODEL), lambda b, i: (b, i, 0))
    per_batch = pl.BlockSpec((1, 1, D_MODEL), lambda b, i: (b, 0, 0))
    return pl.pallas_call(
        _ffn_kernel,
        grid=(nb, s // tm),
        in_specs=[tok, per_batch, per_batch, per_batch, _const_spec((1, D_MODEL)), _const_spec((1, D_MODEL)),
                  _const_spec(w1.shape), _const_spec((3, 2 * D_FF)), _const_spec((1, 2 * D_FF)),
                  _const_spec(w2.shape)],
        out_specs=tok,
        out_shape=jax.ShapeDtypeStruct(x.shape, F32),
        scratch_shapes=[pltpu.VMEM((2 * D_FF // LANES, 8 + tm, LANES), F32), pltpu.VMEM((tm, D_FF), BF16)],
        compiler_params=pltpu.CompilerParams(
            dimension_semantics=("parallel", "arbitrary"), vmem_limit_bytes=VMEM_LIMIT),
        name="conv_ffn",
    )(x, sh, sc, g2, gpre, gpost, w1, cw, cb, w2)


def _reorder_w_in(w):
    def dup(block):
        heads = [block[:, j * HEAD_DIM:(j + 1) * HEAD_DIM] for j in range(N_KV_HEADS)]
        return jnp.concatenate([t for hd in heads for t in (hd, hd)], axis=1)

    k0 = 2 * GM_WIDTH + ATT_WIDTH
    kv = N_KV_HEADS * HEAD_DIM
    return jnp.concatenate(
        [w[:, :k0], dup(w[:, k0:k0 + kv]), dup(w[:, k0 + kv:k0 + 2 * kv]), w[:, k0 + 2 * kv:]], axis=1)


def kernel(x, c, positions, ada_w, ada_b, norm_mix_pre, norm_mix_post, norm_ffn_pre, norm_ffn_post, w_in,
           gm_ln_g, gm_ln_b, gm_ws, gm_bs, attn_sinks, hg_lb_logits, hg_gnorm, proj_a, proj_b, proj_c, w_out,
           ffn_w1, ffn_conv_w, ffn_conv_b, ffn_w2):
    nb = x.shape[0]
    mod = _ada_mod(c, ada_w, ada_b)
    lb_all = _forget_lower_bounds(hg_lb_logits)
    cos_t, sin_t = _rope_tables(positions)
    tri2 = jnp.asarray(np.tile(_HG_TRI_NP, (1, 2)), dtype=BF16)
    lvl2 = jnp.asarray(np.tile(_HG_LVL_NP, (1, 2)))

    def row(v):
        return v.reshape(1, -1)

    for l in range(DEPTH):
        m = mod[l].reshape(nb, 1, 6, D_MODEL)
        sh1, sc1, g1, sh2, sc2, g2 = (m[:, :, t] for t in range(6))
        uv, qkv, hg, lf, gates = _in_call(
            x, sh1, sc1, row(norm_mix_pre[l]), _mxu_weight(_reorder_w_in(w_in[l])), cos_t, sin_t,
            row(lb_all[l]), row(gm_ln_g[l]), row(gm_ln_b[l]))
        bs_b = jnp.broadcast_to(gm_bs[l][:, :, None], (GM_GROUPS, BLK, BLK))
        x = _mix_call(attn_sinks[l], x, uv, qkv, hg, lf, gates, g1, row(norm_mix_post[l]),
                      gm_ws[l], bs_b, tri2, lvl2, row(jnp.tile(hg_gnorm[l], 2)),
                      _mxu_weight(proj_a[l]), _mxu_weight(proj_b[l]), _mxu_weight(proj_c[l]),
                      _mxu_weight(w_out[l]))
        x = _ffn_call(x, sh2, sc2, g2, row(norm_ffn_pre[l]), row(norm_ffn_post[l]),
                      _mxu_weight(ffn_w1[l]), ffn_conv_w[l], row(ffn_conv_b[l]), _mxu_weight(ffn_w2[l]))
    return x
```

```python
import functools

import jax
import jax.numpy as jnp
import numpy as np
from jax import lax
from jax.experimental import pallas as pl
from jax.experimental.pallas import tpu as pltpu

F32 = jnp.float32
BF16 = jnp.bfloat16

D_MODEL = 1024
DEPTH = 4
GM_WIDTH = 512
GM_GROUPS = 4
N_Q_HEADS = 8
N_KV_HEADS = 2
HEAD_DIM = 64
ATT_WIDTH = N_Q_HEADS * HEAD_DIM
ROPE_THETA = 10000.0
MASK_VALUE = -1e30
HG_HEADS = 4
HG_EXPAND = 128
HG_KEY_WIDTH = HG_HEADS * HG_EXPAND
HG_WIDTH = HG_HEADS * 128
HG_MIN_FORGET = 1e-6
D_FF = 2816
EPS = 1e-6

LANES = 128
BLK = 128
MIX_SUB = 4
TM_IN = 256
TM_FFN = 512
FF_CHUNK = 256
VMEM_LIMIT = 56 * 1024 * 1024

C_GU, C_GV, C_Q, C_K, C_V, C_HQ, C_HF, C_HI, C_HG, C_GATE, C_END = (
    0, 512, 1024, 1536, 1792, 2048, 2560, 3072, 3584, 4096, 7168)

HG_LEVEL_HALVES = (64, 32, 16, 8)
HG_DIAG = 8


def _hgrn_tables():
    t = np.arange(BLK)
    tri = (t[None, :] <= t[:, None]).astype(np.float32)
    lvl = np.zeros((BLK, BLK), np.int32)
    for li, h in enumerate(HG_LEVEL_HALVES):
        blk = 2 * h
        second = (t % blk) >= h
        same = (t[:, None] // blk) == (t[None, :] // blk)
        lvl[same & second[:, None] & (~second)[None, :]] = li + 1
    same = (t[:, None] // HG_DIAG) == (t[None, :] // HG_DIAG)
    lvl[same & (t[None, :] <= t[:, None])] = len(HG_LEVEL_HALVES) + 1
    return tri, lvl


_HG_TRI_NP, _HG_LVL_NP = _hgrn_tables()


def _dot(a, b):
    return jnp.dot(a, b, preferred_element_type=F32)


def _dot_nt(a, b):
    return lax.dot_general(a, b, (((1,), (1,)), ((), ())), preferred_element_type=F32)


def _sigmoid(x):
    return 1.0 / (1.0 + jnp.exp(-x))


def _silu(x):
    return x * _sigmoid(x)


def _gelu_tanh(x):
    return 0.5 * x * (1.0 + jnp.tanh(np.sqrt(2.0 / np.pi).astype(np.float32) * (x + 0.044715 * (x * x * x))))


def _rms(x):
    return x * lax.rsqrt(jnp.mean(x * x, axis=-1, keepdims=True) + EPS)


def _const_spec(shape):
    nd = len(shape)
    return pl.BlockSpec(shape, lambda b, i: (0,) * nd, pipeline_mode=pl.Buffered(1))


def _layer_spec(arr, layer):
    shape = arr.shape[1:]
    return pl.BlockSpec((None,) + shape, lambda b, i: (layer,) + (0,) * len(shape), pipeline_mode=pl.Buffered(1))


def _mod_spec(layer, nb, part):
    return pl.BlockSpec((1, 1, D_MODEL), lambda b, i: ((layer * nb + b) * 6 + part, 0, 0))


def _mxu_weight(w):
    w = w.astype(BF16)
    if (w.shape[-1] // LANES) % 2 == 0:
        w = jnp.pad(w, ((0, 0),) * (w.ndim - 1) + ((0, LANES),))
    return w


def _rows(v):
    return v.reshape(v.shape[0], 1, -1)


def _ada_kernel(c_ref, w_ref, b_ref, o_ref):
    c = c_ref[...]
    o_ref[0] = jnp.dot(_silu(c), w_ref[0], preferred_element_type=F32,
                       precision=lax.Precision.HIGHEST) + b_ref[0]


def _ada_mod(c, ada_w, ada_b):
    nb = c.shape[0]
    ncol = ada_w.shape[2] // D_MODEL
    return pl.pallas_call(
        _ada_kernel,
        grid=(DEPTH, ncol),
        in_specs=[pl.BlockSpec((nb, D_MODEL), lambda l, j: (0, 0)),
                  pl.BlockSpec((1, D_MODEL, D_MODEL), lambda l, j: (l, 0, j)),
                  pl.BlockSpec((1, 1, D_MODEL), lambda l, j: (l, 0, j))],
        out_specs=pl.BlockSpec((1, nb, D_MODEL), lambda l, j: (l, 0, j)),
        out_shape=jax.ShapeDtypeStruct((DEPTH, nb, ncol * D_MODEL), F32),
        name="ada_mod",
    )(c, ada_w, ada_b.reshape(DEPTH, 1, -1))


def _lb_kernel(x_ref, o_ref):
    x = x_ref[...]
    e = jnp.exp(x - jnp.max(x, axis=0, keepdims=True))
    p = e / jnp.sum(e, axis=0, keepdims=True)
    run = jnp.zeros_like(p[0:1])
    rows = []
    for l in range(DEPTH):
        run = run + p[l:l + 1]
        rows.append(run - p[0:1])
    o_ref[...] = jnp.concatenate(rows, axis=0)


def _forget_lower_bounds(logits):
    return pl.pallas_call(
        _lb_kernel, out_shape=jax.ShapeDtypeStruct(logits.shape, F32), name="hgrn_lower_bounds",
    )(logits.astype(F32))


def _rope_kernel(pos_ref, inv_ref, sign_ref, cos_ref, sin_ref):
    ang = pos_ref[0] * inv_ref[...]
    cos_ref[0] = jnp.cos(ang)
    sin_ref[0] = jnp.sin(ang) * sign_ref[...]


def _rope_tables(positions):
    nb, s = positions.shape
    half = HEAD_DIM // 2
    inv = ROPE_THETA ** (-jnp.arange(half, dtype=F32) / half)
    inv_t = jnp.tile(inv, LANES // half)[None, :]
    sign = np.where((np.arange(LANES) % HEAD_DIM) < half, -1.0, 1.0).astype(np.float32)[None, :]
    ts = 512
    tab = jax.ShapeDtypeStruct((nb, s, LANES), F32)
    return pl.pallas_call(
        _rope_kernel,
        grid=(nb, s // ts),
        in_specs=[pl.BlockSpec((1, ts, 1), lambda b, i: (b, i, 0)),
                  pl.BlockSpec((1, LANES), lambda b, i: (0, 0)),
                  pl.BlockSpec((1, LANES), lambda b, i: (0, 0))],
        out_specs=[pl.BlockSpec((1, ts, LANES), lambda b, i: (b, i, 0))] * 2,
        out_shape=[tab, tab],
        name="rope_tables",
    )(positions.astype(F32)[..., None], inv_t, jnp.asarray(sign))


def _rope_apply(x, cos, sin_signed, first_half):
    partner = jnp.where(first_half, pltpu.roll(x, LANES - HEAD_DIM // 2, 1), pltpu.roll(x, HEAD_DIM // 2, 1))
    return x * cos + partner * sin_signed


def _in_kernel(x_ref, sh_ref, sc_ref, g_ref, w_ref, cos_ref, sin_ref, lb_ref, lng_ref, lnb_ref,
               uv_ref, qkv_ref, hg_ref, lf_ref, gate_ref):
    x = x_ref[0]
    h = _rms(x) * g_ref[...]
    h = h * (1.0 + sc_ref[0]) + sh_ref[0]
    hb = h.astype(BF16)

    def proj(a, b):
        return _dot(hb, w_ref[:, a:b])

    cos = cos_ref[0]
    sin = sin_ref[0]
    lane = lax.broadcasted_iota(jnp.int32, cos.shape, 1)
    first_half = (lane % HEAD_DIM) < (HEAD_DIM // 2)
    lb = lb_ref[...]

    def epi_u(d):
        uv_ref[0, :, 0:GM_WIDTH] = _gelu_tanh(d).astype(BF16)

    def epi_v(d):
        v = _gelu_tanh(d)
        mu = jnp.mean(v, axis=-1, keepdims=True)
        vc = v - mu
        var = jnp.mean(vc * vc, axis=-1, keepdims=True)
        uv_ref[0, :, GM_WIDTH:2 * GM_WIDTH] = (vc * lax.rsqrt(var + EPS) * lng_ref[...] + lnb_ref[...]).astype(BF16)

    def epi_rope(base, scale):
        def f(d):
            for gidx in range(d.shape[1] // LANES):
                r = _rope_apply(d[:, gidx * LANES:(gidx + 1) * LANES], cos, sin, first_half)
                if scale != 1.0:
                    r = r * scale
                qkv_ref[0, :, base + gidx * LANES:base + (gidx + 1) * LANES] = r.astype(BF16)
        return f

    def epi_vdup(d):
        qkv_ref[0, :, C_V - C_Q:C_HQ - C_Q] = d.astype(BF16)

    def epi_hq(d):
        hg_ref[0, :, 0:512] = (_silu(d) * (HG_EXPAND ** -0.5)).astype(BF16)

    def epi_hf(fl):
        forget = lb + (1.0 - lb) * _sigmoid(fl)
        lf_ref[0] = jnp.log(jnp.maximum(forget, HG_MIN_FORGET))
        hg_ref[0, :, 512:1024] = ((1.0 - lb) * _sigmoid(-fl)).astype(BF16)

    def epi_hi(d):
        hg_ref[0, :, 1024:1536] = d.astype(BF16)

    def epi_hgate(d):
        hg_ref[0, :, 1536:2048] = _silu(d).astype(BF16)

    def epi_gate(gidx):
        def f(d):
            gate_ref[0, :, gidx * 512:(gidx + 1) * 512] = _sigmoid(d).astype(BF16)
        return f

    stages = [(C_GU, C_GV, epi_u), (C_GV, C_Q, epi_v), (C_Q, C_K, epi_rope(0, HEAD_DIM ** -0.5)),
              (C_K, C_V, epi_rope(C_K - C_Q, 1.0)), (C_V, C_HQ, epi_vdup), (C_HQ, C_HF, epi_hq),
              (C_HF, C_HI, epi_hf), (C_HG, C_GATE, epi_hgate)]
    stages += [(C_GATE + g * 512, C_GATE + (g + 1) * 512, epi_gate(g)) for g in range(3 * D_MODEL // 512)]
    stages += [(C_HI, C_HG, epi_hi)]
    for a, b, epi in stages:
        epi(proj(a, b))


def _in_call(layer, x, mod, g, w, cos_t, sin_t, lb, lng, lnb):
    nb, s, _ = x.shape
    tm = TM_IN

    def tok(width):
        return pl.BlockSpec((1, tm, width), lambda b, i: (b, i, 0))

    def out(width, dt):
        return jax.ShapeDtypeStruct((nb, s, width), dt)

    return pl.pallas_call(
        _in_kernel,
        grid=(nb, s // tm),
        in_specs=[tok(D_MODEL), _mod_spec(layer, nb, 0), _mod_spec(layer, nb, 1), _layer_spec(g, layer),
                  _layer_spec(w, layer), tok(LANES), tok(LANES),
                  _layer_spec(lb, layer), _layer_spec(lng, layer), _layer_spec(lnb, layer)],
        out_specs=[tok(1024), tok(1024), tok(2048), tok(512), tok(3072)],
        out_shape=[out(1024, BF16), out(1024, BF16), out(2048, BF16), out(512, F32), out(3072, BF16)],
        compiler_params=pltpu.CompilerParams(
            dimension_semantics=("parallel", "parallel"), vmem_limit_bytes=VMEM_LIMIT),
        name="mixer_in",
    )(x, mod, mod, g, w, cos_t, sin_t, lb, lng, lnb)


def _spatial_gating(uv, ws_bf, bs_ref):
    outs = []
    for g in range(GM_GROUPS):
        sl = slice(g * LANES, (g + 1) * LANES)
        s = _dot(ws_bf[g], uv[:, GM_WIDTH + g * LANES:GM_WIDTH + (g + 1) * LANES]) + bs_ref[g]
        outs.append(uv[:, sl].astype(F32) * s)
    return jnp.concatenate(outs, axis=1)


def _window_attention(qkv, k_prev, v_prev, sink_ref, sink_base, first_key):
    nk = 2 * BLK
    qi = lax.broadcasted_iota(jnp.int32, (BLK, nk), 0)
    kj = lax.broadcasted_iota(jnp.int32, (BLK, nk), 1)
    valid = (kj > qi) & (kj <= qi + BLK) & (kj >= first_key)
    row_t = lax.broadcasted_iota(jnp.int32, (LANES, nk), 0)
    low_t = row_t < HEAD_DIM
    lane_o = lax.broadcasted_iota(jnp.int32, (BLK, LANES), 1)
    low_o = lane_o < HEAD_DIM
    pairs_per_kv = (N_Q_HEADS // N_KV_HEADS) // 2
    outs, k_new, v_new = [], [], []
    for j in range(N_KV_HEADS):
        k_cur = qkv[:, ATT_WIDTH + j * LANES:ATT_WIDTH + (j + 1) * LANES]
        v_cur = qkv[:, ATT_WIDTH + (N_KV_HEADS + j) * LANES:ATT_WIDTH + (N_KV_HEADS + j + 1) * LANES]
        kt = jnp.concatenate([k_prev[j], k_cur], axis=0).astype(F32).T
        zero = jnp.zeros_like(kt)
        k_sel = jnp.concatenate([jnp.where(low_t, kt, zero), jnp.where(low_t, zero, kt)], axis=1).astype(BF16)
        vd = jnp.concatenate([v_prev[j], v_cur], axis=0)
        q4 = jnp.concatenate([qkv[:, (j * pairs_per_kv + pp) * LANES:(j * pairs_per_kv + pp + 1) * LANES]
                              for pp in range(pairs_per_kv)], axis=0)
        s_all = _dot(q4, k_sel)
        probs, dens = [], []
        for pp in range(pairs_per_kv):
            for par in range(2):
                sink = sink_ref[sink_base + 2 * (j * pairs_per_kv + pp) + par]
                s = jnp.where(valid, s_all[pp * BLK:(pp + 1) * BLK, par * nk:(par + 1) * nk], MASK_VALUE)
                m = jnp.maximum(jnp.max(s, axis=-1, keepdims=True), sink)
                p = jnp.exp(s - m)
                dens.append(jnp.sum(p, axis=-1, keepdims=True) + jnp.exp(sink - m))
                probs.append(p.astype(BF16))
        o_all = _dot(jnp.concatenate(probs, axis=0), vd)
        for pp in range(pairs_per_kv):
            halves = [o_all[(2 * pp + par) * BLK:(2 * pp + par + 1) * BLK] * (1.0 / dens[2 * pp + par])
                      for par in range(2)]
            outs.append(jnp.where(low_o, halves[0], halves[1]))
        k_new.append(k_cur)
        v_new.append(v_cur)
    return jnp.concatenate(outs, axis=1), k_new, v_new


def _level_exponents(b, h):
    pieces = []
    for k in range(BLK // (2 * h)):
        blk = b[k * 2 * h:(k + 1) * 2 * h]
        r = blk[h - 1:h]
        pieces.append(r - blk[:h])
        pieces.append(blk[h:] - r)
    return jnp.concatenate(pieces, axis=0)


def _block_diag(x2, top_mask):
    zero = jnp.zeros_like(x2)
    return jnp.concatenate([jnp.where(top_mask, x2, zero), jnp.where(top_mask, zero, x2)], axis=0)


def _block_diag_t(x2):
    xt = x2.T
    row = lax.broadcasted_iota(jnp.int32, xt.shape, 0)
    top = row < LANES
    zero = jnp.zeros_like(xt)
    return jnp.concatenate([jnp.where(top, xt, zero), jnp.where(top, zero, xt)], axis=1).astype(BF16)


def _hgrn2_pair(q, k, v, gs, lf, st, tri2, lvl2, gn2):
    n_lv = len(HG_LEVEL_HALVES)
    w2 = 2 * LANES
    lane = lax.broadcasted_iota(jnp.int32, (BLK, w2), 1)
    first = lane < LANES
    lane_s = lax.broadcasted_iota(jnp.int32, (w2, w2), 1)
    row_s = lax.broadcasted_iota(jnp.int32, (w2, w2), 0)
    diag_blocks = (lane_s < LANES) == (row_s < LANES)
    hi = lf.astype(BF16)
    lo = (lf - hi.astype(F32)).astype(BF16)
    b = _dot(tri2, jnp.concatenate([hi, lo], axis=0))
    o = _dot_nt((q * jnp.exp(b)).astype(BF16), st.astype(BF16))
    att = jnp.zeros((BLK, w2), F32)
    for li, h in enumerate(HG_LEVEL_HALVES):
        e = jnp.exp(_level_exponents(b, h))
        a = _dot((q * e).astype(BF16), _block_diag_t(k * e))
        att = jnp.where(lvl2 == li + 1, a, att)
    b3 = b.reshape(BLK // HG_DIAG, HG_DIAG, w2)
    xd = (b3 - b3[:, HG_DIAG // 2 - 1:HG_DIAG // 2, :]).reshape(BLK, w2)
    a = _dot((q * jnp.exp(xd)).astype(BF16), _block_diag_t(k * jnp.exp(-xd)))
    att = jnp.where(lvl2 == n_lv + 1, a, att)
    o = o + _dot(att.astype(BF16), _block_diag(v, first))
    b_last = b[BLK - 1:BLK]
    k_end = (k * jnp.exp(b_last - b)).astype(BF16)
    vt = v.astype(F32).T.astype(BF16)
    st_new = jnp.where(diag_blocks, st * jnp.exp(b_last) + _dot(vt, k_end), 0.0)
    outs = [_rms(o[:, t * LANES:(t + 1) * LANES]) for t in range(2)]
    return jnp.concatenate(outs, axis=1) * gn2 * gs, st_new


def _mix_kernel(sink_ref, x_ref, uv_ref, qkv_ref, hg_ref, lf_ref, gate_ref, g1_ref, gpost_ref,
                ws_ref, bs_ref, tri_ref, lvl_ref, gn_ref, pa_ref, pb_ref, pc_ref, wo_ref,
                o_ref, kprev_ref, vprev_ref, state_ref, *, layer):
    step = pl.program_id(1)

    @pl.when(step == 0)
    def _():
        kprev_ref[...] = jnp.zeros_like(kprev_ref)
        vprev_ref[...] = jnp.zeros_like(vprev_ref)
        state_ref[...] = jnp.zeros_like(state_ref)

    row = lax.broadcasted_iota(jnp.int32, (BLK, BLK), 0)
    col = lax.broadcasted_iota(jnp.int32, (BLK, BLK), 1)
    ws_bf = [jnp.where(col <= row, ws_ref[g], 0.0).astype(BF16) for g in range(GM_GROUPS)]
    tri2 = tri_ref[...]
    lvl2 = lvl_ref[...]
    gn2 = gn_ref[...]
    k_prev = [kprev_ref[j] for j in range(N_KV_HEADS)]
    v_prev = [vprev_ref[j] for j in range(N_KV_HEADS)]
    states = [state_ref[p] for p in range(HG_HEADS // 2)]
    ya, yb, yc = [], [], []
    for sb in range(MIX_SUB):
        rows = slice(sb * BLK, (sb + 1) * BLK)
        ya.append(_spatial_gating(uv_ref[0, rows, :], ws_bf, bs_ref))
        first_key = jnp.where(step > 0, 0, BLK) if sb == 0 else 0
        o_b, k_prev, v_prev = _window_attention(qkv_ref[0, rows, :], k_prev, v_prev, sink_ref,
                                                layer * N_Q_HEADS, first_key)
        yb.append(o_b)
        hg = hg_ref[0, rows, :]
        lf = lf_ref[0, rows, :]
        heads = []
        for p in range(HG_HEADS // 2):
            def part(base):
                return hg[:, base + p * 2 * LANES:base + (p + 1) * 2 * LANES]
            o_c, states[p] = _hgrn2_pair(
                part(0).astype(F32), part(512).astype(F32), part(1024), part(1536).astype(F32),
                lf[:, p * 2 * LANES:(p + 1) * 2 * LANES], states[p], tri2, lvl2, gn2)
            heads.append(o_c)
        yc.append(jnp.concatenate(heads, axis=1))
    for j in range(N_KV_HEADS):
        kprev_ref[j] = k_prev[j]
        vprev_ref[j] = v_prev[j]
    for p in range(HG_HEADS // 2):
        state_ref[p] = states[p]
    y_a = jnp.concatenate(ya, axis=0).astype(BF16)
    y_b = jnp.concatenate(yb, axis=0).astype(BF16)
    y_c = jnp.concatenate(yc, axis=0).astype(BF16)
    gates = gate_ref[0]
    merged = (gates[:, 0:D_MODEL].astype(F32) * _dot(y_a, pa_ref[:, 0:D_MODEL])
              + gates[:, D_MODEL:2 * D_MODEL].astype(F32) * _dot(y_b, pb_ref[:, 0:D_MODEL])
              + gates[:, 2 * D_MODEL:3 * D_MODEL].astype(F32) * _dot(y_c, pc_ref[:, 0:D_MODEL]))
    y = _dot(merged.astype(BF16), wo_ref[:, 0:D_MODEL])
    o_ref[0] = x_ref[0] + g1_ref[0] * (_rms(y) * gpost_ref[...])


def _mix_call(layer, sinks, x, uv, qkv, hg, lf, gates, mod, gpost, ws, bs_b, tri, lvl, gn, pa, pb, pc, wo):
    nb, s, _ = x.shape
    tb = MIX_SUB * BLK

    def tok(width):
        return pl.BlockSpec((1, tb, width), lambda b, i: (b, i, 0))

    return pl.pallas_call(
        functools.partial(_mix_kernel, layer=layer),
        grid=(nb, s // tb),
        in_specs=[pl.BlockSpec(memory_space=pltpu.SMEM),
                  tok(D_MODEL), tok(1024), tok(1024), tok(2048), tok(512), tok(3072),
                  _mod_spec(layer, nb, 2), _layer_spec(gpost, layer),
                  _layer_spec(ws, layer), _layer_spec(bs_b, layer),
                  _const_spec((BLK, 2 * BLK)), _const_spec((BLK, 2 * BLK)), _layer_spec(gn, layer),
                  _layer_spec(pa, layer), _layer_spec(pb, layer), _layer_spec(pc, layer), _layer_spec(wo, layer)],
        out_specs=tok(D_MODEL),
        out_shape=jax.ShapeDtypeStruct(x.shape, F32),
        scratch_shapes=[pltpu.VMEM((N_KV_HEADS, BLK, LANES), BF16),
                        pltpu.VMEM((N_KV_HEADS, BLK, LANES), BF16),
                        pltpu.VMEM((HG_HEADS // 2, 2 * LANES, 2 * HG_EXPAND), F32)],
        compiler_params=pltpu.CompilerParams(
            dimension_semantics=("parallel", "arbitrary"), vmem_limit_bytes=VMEM_LIMIT),
        name="mixer_core",
    )(sinks, x, uv, qkv, hg, lf, gates, mod, gpost, ws, bs_b, tri, lvl, gn, pa, pb, pc, wo)


def _ffn_kernel(x_ref, sh_ref, sc_ref, gate_ref, gpre_ref, gpost_ref, w1_ref, cw_ref, cb_ref, w2_ref,
                o_ref, a_ref, hid_ref):
    step = pl.program_id(1)
    tm = x_ref.shape[1]

    @pl.when(step == 0)
    def _():
        a_ref[:, 0:8, :] = jnp.zeros((a_ref.shape[0], 8, LANES), F32)

    x = x_ref[0]
    h = _rms(x) * gpre_ref[...]
    hb = (h * (1.0 + sc_ref[0]) + sh_ref[0]).astype(BF16)

    def conv(col):
        a = _dot(hb, w1_ref[:, col:col + FF_CHUNK])
        taps = []
        for t in range(FF_CHUNK // LANES):
            a_ref[col // LANES + t, 8:8 + tm, :] = a[:, t * LANES:(t + 1) * LANES]
        for t in range(FF_CHUNK // LANES):
            slab = col // LANES + t
            sl = slice(col + t * LANES, col + (t + 1) * LANES)
            cw = cw_ref[:, sl]
            taps.append(cw[0:1] * a_ref[slab, 6:6 + tm, :] + cw[1:2] * a_ref[slab, 7:7 + tm, :]
                        + cw[2:3] * a[:, t * LANES:(t + 1) * LANES] + cb_ref[:, sl])
            a_ref[slab, 0:8, :] = a[tm - 8:tm, t * LANES:(t + 1) * LANES]
        return jnp.concatenate(taps, axis=1)

    for c in range(D_FF // FF_CHUNK):
        gate = conv(c * FF_CHUNK)
        val = conv(D_FF + c * FF_CHUNK)
        hid_ref[:, c * FF_CHUNK:(c + 1) * FF_CHUNK] = (_silu(gate) * val).astype(BF16)
    y = _dot(hid_ref[...], w2_ref[:, 0:D_MODEL])
    o_ref[0] = x + gate_ref[0] * (_rms(y) * gpost_ref[...])


def _ffn_call(layer, x, mod, gpre, gpost, w1, cw, cb, w2):
    nb, s, _ = x.shape
    tm = TM_FFN
    tok = pl.BlockSpec((1, tm, D_MODEL), lambda b, i: (b, i, 0))
    return pl.pallas_call(
        _ffn_kernel,
        grid=(nb, s // tm),
        in_specs=[tok, _mod_spec(layer, nb, 3), _mod_spec(layer, nb, 4), _mod_spec(layer, nb, 5),
                  _layer_spec(gpre, layer), _layer_spec(gpost, layer),
                  _layer_spec(w1, layer), _layer_spec(cw, layer), _layer_spec(cb, layer), _layer_spec(w2, layer)],
        out_specs=tok,
        out_shape=jax.ShapeDtypeStruct(x.shape, F32),
        scratch_shapes=[pltpu.VMEM((2 * D_FF // LANES, 8 + tm, LANES), F32), pltpu.VMEM((tm, D_FF), BF16)],
        compiler_params=pltpu.CompilerParams(
            dimension_semantics=("parallel", "arbitrary"), vmem_limit_bytes=VMEM_LIMIT),
        name="conv_ffn",
    )(x, mod, mod, mod, gpre, gpost, w1, cw, cb, w2)


def _reorder_w_in(w):
    def dup(block):
        heads = [block[..., j * HEAD_DIM:(j + 1) * HEAD_DIM] for j in range(N_KV_HEADS)]
        return [t for hd in heads for t in (hd, hd)]

    k0 = 2 * GM_WIDTH + ATT_WIDTH
    kv = N_KV_HEADS * HEAD_DIM
    return jnp.concatenate(
        [w[..., :k0]] + dup(w[..., k0:k0 + kv]) + dup(w[..., k0 + kv:k0 + 2 * kv]) + [w[..., k0 + 2 * kv:]],
        axis=-1)


def kernel(x, c, positions, ada_w, ada_b, norm_mix_pre, norm_mix_post, norm_ffn_pre, norm_ffn_post, w_in,
           gm_ln_g, gm_ln_b, gm_ws, gm_bs, attn_sinks, hg_lb_logits, hg_gnorm, proj_a, proj_b, proj_c, w_out,
           ffn_w1, ffn_conv_w, ffn_conv_b, ffn_w2):
    mod = _ada_mod(c, ada_w, ada_b).reshape(-1, 1, D_MODEL)
    lb_all = _rows(_forget_lower_bounds(hg_lb_logits))
    cos_t, sin_t = _rope_tables(positions)
    tri2 = jnp.asarray(np.tile(_HG_TRI_NP, (1, 2)), dtype=BF16)
    lvl2 = jnp.asarray(np.tile(_HG_LVL_NP, (1, 2)))
    w_in_b = _mxu_weight(_reorder_w_in(w_in.astype(BF16)))
    pa, pb, pc, wo = (_mxu_weight(t) for t in (proj_a, proj_b, proj_c, w_out))
    w1, w2 = _mxu_weight(ffn_w1), _mxu_weight(ffn_w2)
    bs_b = jnp.broadcast_to(gm_bs[:, :, :, None], (DEPTH, GM_GROUPS, BLK, BLK))
    sinks = attn_sinks.reshape(-1)
    gn2 = _rows(jnp.tile(hg_gnorm, (1, 2)))
    g_mix_pre, g_mix_post = _rows(norm_mix_pre), _rows(norm_mix_post)
    g_ffn_pre, g_ffn_post = _rows(norm_ffn_pre), _rows(norm_ffn_post)
    ln_g, ln_b, conv_b = _rows(gm_ln_g), _rows(gm_ln_b), _rows(ffn_conv_b)

    for l in range(DEPTH):
        uv, qkv, hg, lf, gates = _in_call(l, x, mod, g_mix_pre, w_in_b, cos_t, sin_t, lb_all, ln_g, ln_b)
        x = _mix_call(l, sinks, x, uv, qkv, hg, lf, gates, mod, g_mix_post, gm_ws, bs_b, tri2, lvl2, gn2,
                      pa, pb, pc, wo)
        x = _ffn_call(l, x, mod, g_ffn_pre, g_ffn_post, w1, ffn_conv_w, conv_b, w2)
    return x
```

```python
import functools

import jax
import jax.numpy as jnp
import numpy as np
from jax import lax
from jax.experimental import pallas as pl
from jax.experimental.pallas import tpu as pltpu

F32 = jnp.float32
BF16 = jnp.bfloat16

D_MODEL = 1024
DEPTH = 4
GM_WIDTH = 512
GM_GROUPS = 4
N_Q_HEADS = 8
N_KV_HEADS = 2
HEAD_DIM = 64
ATT_WIDTH = N_Q_HEADS * HEAD_DIM
ROPE_THETA = 10000.0
MASK_VALUE = -1e30
HG_HEADS = 4
HG_EXPAND = 128
HG_KEY_WIDTH = HG_HEADS * HG_EXPAND
HG_WIDTH = HG_HEADS * 128
HG_MIN_FORGET = 1e-6
D_FF = 2816
EPS = 1e-6

LANES = 128
BLK = 128
MIX_SUB = 4
TM_IN = 256
TM_FFN = 512
FF_CHUNK = 256
VMEM_LIMIT = 56 * 1024 * 1024

C_GU, C_GV, C_Q, C_K, C_V, C_HQ, C_HF, C_HI, C_HG, C_GATE, C_END = (
    0, 512, 1024, 1536, 1792, 2048, 2560, 3072, 3584, 4096, 7168)

HG_LEVEL_HALVES = (64, 32, 16, 8)
HG_DIAG = 8


def _hgrn_tables():
    t = np.arange(BLK)
    tri = (t[None, :] <= t[:, None]).astype(np.float32)
    lvl = np.zeros((BLK, BLK), np.int32)
    for li, h in enumerate(HG_LEVEL_HALVES):
        blk = 2 * h
        second = (t % blk) >= h
        same = (t[:, None] // blk) == (t[None, :] // blk)
        lvl[same & second[:, None] & (~second)[None, :]] = li + 1
    same = (t[:, None] // HG_DIAG) == (t[None, :] // HG_DIAG)
    lvl[same & (t[None, :] <= t[:, None])] = len(HG_LEVEL_HALVES) + 1
    return tri, lvl


_HG_TRI_NP, _HG_LVL_NP = _hgrn_tables()


def _dot(a, b):
    return jnp.dot(a, b, preferred_element_type=F32)


def _dot_nt(a, b):
    return lax.dot_general(a, b, (((1,), (1,)), ((), ())), preferred_element_type=F32)


def _sigmoid(x):
    return 1.0 / (1.0 + jnp.exp(-x))


def _silu(x):
    return x * _sigmoid(x)


def _gelu_tanh(x):
    return 0.5 * x * (1.0 + jnp.tanh(np.sqrt(2.0 / np.pi).astype(np.float32) * (x + 0.044715 * (x * x * x))))


def _rms(x):
    return x * lax.rsqrt(jnp.mean(x * x, axis=-1, keepdims=True) + EPS)


def _const_spec(shape):
    nd = len(shape)
    return pl.BlockSpec(shape, lambda b, i: (0,) * nd, pipeline_mode=pl.Buffered(1))


def _layer_spec(arr, layer):
    shape = arr.shape[1:]
    return pl.BlockSpec((None,) + shape, lambda b, i: (layer,) + (0,) * len(shape), pipeline_mode=pl.Buffered(1))


def _mod_spec(layer, nb, part):
    return pl.BlockSpec((1, 1, D_MODEL), lambda b, i: ((layer * nb + b) * 6 + part, 0, 0))


def _weight_kernel(w_ref, o_ref, *, pieces):
    end = 0
    for src, dst, width in pieces:
        o_ref[:, dst:dst + width] = w_ref[:, src:src + width].astype(BF16)
        end = max(end, dst + width)
    if end < o_ref.shape[1]:
        o_ref[:, end:] = jnp.zeros((o_ref.shape[0], o_ref.shape[1] - end), BF16)


def _mxu_weight(w, pieces=None):
    depth, k, n = w.shape
    if pieces is None:
        pieces = ((0, 0, n),)
    width = max(dst + wd for _, dst, wd in pieces)
    if (width // LANES) % 2 == 0:
        width += LANES
    tk = 256
    return pl.pallas_call(
        functools.partial(_weight_kernel, pieces=pieces),
        grid=(depth, k // tk),
        in_specs=[pl.BlockSpec((None, tk, n), lambda l, i: (l, i, 0))],
        out_specs=pl.BlockSpec((None, tk, width), lambda l, i: (l, i, 0)),
        out_shape=jax.ShapeDtypeStruct((depth, k, width), BF16),
        compiler_params=pltpu.CompilerParams(
            dimension_semantics=("parallel", "parallel"), vmem_limit_bytes=VMEM_LIMIT),
        name="weight_prep",
    )(w)


def _rows(v):
    return v.reshape(v.shape[0], 1, -1)


def _ada_kernel(c_ref, w_ref, b_ref, o_ref):
    c = c_ref[...]
    o_ref[0] = _dot(_silu(c).astype(BF16), w_ref[0].astype(BF16)) + b_ref[0]


def _ada_mod(c, ada_w, ada_b):
    nb = c.shape[0]
    ncol = ada_w.shape[2] // D_MODEL
    return pl.pallas_call(
        _ada_kernel,
        grid=(DEPTH, ncol),
        in_specs=[pl.BlockSpec((nb, D_MODEL), lambda l, j: (0, 0)),
                  pl.BlockSpec((1, D_MODEL, D_MODEL), lambda l, j: (l, 0, j)),
                  pl.BlockSpec((1, 1, D_MODEL), lambda l, j: (l, 0, j))],
        out_specs=pl.BlockSpec((1, nb, D_MODEL), lambda l, j: (l, 0, j)),
        out_shape=jax.ShapeDtypeStruct((DEPTH, nb, ncol * D_MODEL), F32),
        name="ada_mod",
    )(c, ada_w, ada_b.reshape(DEPTH, 1, -1))


def _lb_kernel(x_ref, o_ref):
    x = x_ref[...]
    e = jnp.exp(x - jnp.max(x, axis=0, keepdims=True))
    p = e / jnp.sum(e, axis=0, keepdims=True)
    run = jnp.zeros_like(p[0:1])
    rows = []
    for l in range(DEPTH):
        run = run + p[l:l + 1]
        rows.append(run - p[0:1])
    o_ref[...] = jnp.concatenate(rows, axis=0)


def _forget_lower_bounds(logits):
    return pl.pallas_call(
        _lb_kernel, out_shape=jax.ShapeDtypeStruct(logits.shape, F32), name="hgrn_lower_bounds",
    )(logits.astype(F32))


def _rope_kernel(pos_ref, inv_ref, sign_ref, cos_ref, sin_ref):
    ang = pos_ref[0] * inv_ref[...]
    cos_ref[0] = jnp.cos(ang)
    sin_ref[0] = jnp.sin(ang) * sign_ref[...]


def _rope_tables(positions):
    nb, s = positions.shape
    half = HEAD_DIM // 2
    inv = ROPE_THETA ** (-jnp.arange(half, dtype=F32) / half)
    inv_t = jnp.tile(inv, LANES // half)[None, :]
    sign = np.where((np.arange(LANES) % HEAD_DIM) < half, -1.0, 1.0).astype(np.float32)[None, :]
    ts = 512
    tab = jax.ShapeDtypeStruct((nb, s, LANES), F32)
    return pl.pallas_call(
        _rope_kernel,
        grid=(nb, s // ts),
        in_specs=[pl.BlockSpec((1, ts, 1), lambda b, i: (b, i, 0)),
                  pl.BlockSpec((1, LANES), lambda b, i: (0, 0)),
                  pl.BlockSpec((1, LANES), lambda b, i: (0, 0))],
        out_specs=[pl.BlockSpec((1, ts, LANES), lambda b, i: (b, i, 0))] * 2,
        out_shape=[tab, tab],
        name="rope_tables",
    )(positions.astype(F32)[..., None], inv_t, jnp.asarray(sign))


def _rope_apply(x, cos, sin_signed, first_half):
    partner = jnp.where(first_half, pltpu.roll(x, LANES - HEAD_DIM // 2, 1), pltpu.roll(x, HEAD_DIM // 2, 1))
    return x * cos + partner * sin_signed


def _in_kernel(x_ref, sh_ref, sc_ref, g_ref, w_ref, cos_ref, sin_ref, lb_ref, lng_ref, lnb_ref,
               uv_ref, qkv_ref, hg_ref, lf_ref, gate_ref):
    x = x_ref[0]
    h = _rms(x) * g_ref[...]
    h = h * (1.0 + sc_ref[0]) + sh_ref[0]
    hb = h.astype(BF16)

    def proj(a, b):
        return _dot(hb, w_ref[:, a:b])

    cos = cos_ref[0]
    sin = sin_ref[0]
    lane = lax.broadcasted_iota(jnp.int32, cos.shape, 1)
    first_half = (lane % HEAD_DIM) < (HEAD_DIM // 2)
    lb = lb_ref[...]

    def epi_u(d):
        uv_ref[0, :, 0:GM_WIDTH] = _gelu_tanh(d).astype(BF16)

    def epi_v(d):
        v = _gelu_tanh(d)
        mu = jnp.mean(v, axis=-1, keepdims=True)
        vc = v - mu
        var = jnp.mean(vc * vc, axis=-1, keepdims=True)
        uv_ref[0, :, GM_WIDTH:2 * GM_WIDTH] = (vc * lax.rsqrt(var + EPS) * lng_ref[...] + lnb_ref[...]).astype(BF16)

    def epi_rope(base, scale):
        def f(d):
            for gidx in range(d.shape[1] // LANES):
                r = _rope_apply(d[:, gidx * LANES:(gidx + 1) * LANES], cos, sin, first_half)
                if scale != 1.0:
                    r = r * scale
                qkv_ref[0, :, base + gidx * LANES:base + (gidx + 1) * LANES] = r.astype(BF16)
        return f

    def epi_vdup(d):
        qkv_ref[0, :, C_V - C_Q:C_HQ - C_Q] = d.astype(BF16)

    def epi_hq(d):
        hg_ref[0, :, 0:512] = (_silu(d) * (HG_EXPAND ** -0.5)).astype(BF16)

    def epi_hf(fl):
        forget = lb + (1.0 - lb) * _sigmoid(fl)
        lf_ref[0] = jnp.log(jnp.maximum(forget, HG_MIN_FORGET))
        hg_ref[0, :, 512:1024] = ((1.0 - lb) * _sigmoid(-fl)).astype(BF16)

    def epi_hi(d):
        hg_ref[0, :, 1024:1536] = d.astype(BF16)

    def epi_hgate(d):
        hg_ref[0, :, 1536:2048] = _silu(d).astype(BF16)

    def epi_gate(gidx):
        def f(d):
            gate_ref[0, :, gidx * 512:(gidx + 1) * 512] = _sigmoid(d).astype(BF16)
        return f

    stages = [(C_GU, C_GV, epi_u), (C_GV, C_Q, epi_v), (C_Q, C_K, epi_rope(0, HEAD_DIM ** -0.5)),
              (C_K, C_V, epi_rope(C_K - C_Q, 1.0)), (C_V, C_HQ, epi_vdup), (C_HQ, C_HF, epi_hq),
              (C_HF, C_HI, epi_hf), (C_HG, C_GATE, epi_hgate)]
    stages += [(C_GATE + g * 512, C_GATE + (g + 1) * 512, epi_gate(g)) for g in range(3 * D_MODEL // 512)]
    stages += [(C_HI, C_HG, epi_hi)]
    for a, b, epi in stages:
        epi(proj(a, b))


def _in_call(layer, x, mod, g, w, cos_t, sin_t, lb, lng, lnb):
    nb, s, _ = x.shape
    tm = TM_IN

    def tok(width):
        return pl.BlockSpec((1, tm, width), lambda b, i: (b, i, 0))

    def out(width, dt):
        return jax.ShapeDtypeStruct((nb, s, width), dt)

    return pl.pallas_call(
        _in_kernel,
        grid=(nb, s // tm),
        in_specs=[tok(D_MODEL), _mod_spec(layer, nb, 0), _mod_spec(layer, nb, 1), _layer_spec(g, layer),
                  _layer_spec(w, layer), tok(LANES), tok(LANES),
                  _layer_spec(lb, layer), _layer_spec(lng, layer), _layer_spec(lnb, layer)],
        out_specs=[tok(1024), tok(1024), tok(2048), tok(512), tok(3072)],
        out_shape=[out(1024, BF16), out(1024, BF16), out(2048, BF16), out(512, F32), out(3072, BF16)],
        compiler_params=pltpu.CompilerParams(
            dimension_semantics=("parallel", "parallel"), vmem_limit_bytes=VMEM_LIMIT),
        name="mixer_in",
    )(x, mod, mod, g, w, cos_t, sin_t, lb, lng, lnb)


def _spatial_gating(uv, ws_bf, bs_ref):
    outs = []
    for g in range(GM_GROUPS):
        sl = slice(g * LANES, (g + 1) * LANES)
        s = _dot(ws_bf[g], uv[:, GM_WIDTH + g * LANES:GM_WIDTH + (g + 1) * LANES]) + bs_ref[g]
        outs.append(uv[:, sl].astype(F32) * s)
    return jnp.concatenate(outs, axis=1)


def _window_attention(qkv, k_prev, v_prev, sink_ref, sink_base, first_key):
    nk = 2 * BLK
    qi = lax.broadcasted_iota(jnp.int32, (BLK, nk), 0)
    kj = lax.broadcasted_iota(jnp.int32, (BLK, nk), 1)
    valid = (kj > qi) & (kj <= qi + BLK) & (kj >= first_key)
    row_t = lax.broadcasted_iota(jnp.int32, (LANES, nk), 0)
    low_t = row_t < HEAD_DIM
    lane_o = lax.broadcasted_iota(jnp.int32, (BLK, LANES), 1)
    low_o = lane_o < HEAD_DIM
    pairs_per_kv = (N_Q_HEADS // N_KV_HEADS) // 2
    outs, k_new, v_new = [], [], []
    for j in range(N_KV_HEADS):
        k_cur = qkv[:, ATT_WIDTH + j * LANES:ATT_WIDTH + (j + 1) * LANES]
        v_cur = qkv[:, ATT_WIDTH + (N_KV_HEADS + j) * LANES:ATT_WIDTH + (N_KV_HEADS + j + 1) * LANES]
        kt = jnp.concatenate([k_prev[j], k_cur], axis=0).T
        zero = jnp.zeros_like(kt)
        k_sel = jnp.concatenate([jnp.where(low_t, kt, zero), jnp.where(low_t, zero, kt)], axis=1)
        vd = jnp.concatenate([jnp.concatenate([v_prev[j], v_cur], axis=0), jnp.ones((nk, LANES), BF16)], axis=1)
        q4 = jnp.concatenate([qkv[:, (j * pairs_per_kv + pp) * LANES:(j * pairs_per_kv + pp + 1) * LANES]
                              for pp in range(pairs_per_kv)], axis=0)
        s_all = _dot(q4, k_sel)
        probs, sink_terms = [], []
        for pp in range(pairs_per_kv):
            for par in range(2):
                sink = sink_ref[sink_base + 2 * (j * pairs_per_kv + pp) + par]
                s = jnp.where(valid, s_all[pp * BLK:(pp + 1) * BLK, par * nk:(par + 1) * nk], MASK_VALUE)
                m = jnp.maximum(jnp.max(s, axis=-1, keepdims=True), sink)
                probs.append(jnp.exp(s - m).astype(BF16))
                sink_terms.append(jnp.exp(sink - m))
        o_all = _dot(jnp.concatenate(probs, axis=0), vd)
        for pp in range(pairs_per_kv):
            halves = []
            for par in range(2):
                blk = o_all[(2 * pp + par) * BLK:(2 * pp + par + 1) * BLK]
                halves.append(blk[:, :LANES] / (blk[:, LANES:] + sink_terms[2 * pp + par]))
            outs.append(jnp.where(low_o, halves[0], halves[1]))
        k_new.append(k_cur)
        v_new.append(v_cur)
    return jnp.concatenate(outs, axis=1), k_new, v_new


def _level_exponents(b, h):
    pieces = []
    for k in range(BLK // (2 * h)):
        blk = b[k * 2 * h:(k + 1) * 2 * h]
        r = blk[h - 1:h]
        pieces.append(r - blk[:h])
        pieces.append(blk[h:] - r)
    return jnp.concatenate(pieces, axis=0)


def _block_diag(x2, top_mask):
    zero = jnp.zeros_like(x2)
    return jnp.concatenate([jnp.where(top_mask, x2, zero), jnp.where(top_mask, zero, x2)], axis=0)


def _block_diag_t(x2):
    xt = x2.astype(BF16).T
    row = lax.broadcasted_iota(jnp.int32, xt.shape, 0)
    top = row < LANES
    zero = jnp.zeros_like(xt)
    return jnp.concatenate([jnp.where(top, xt, zero), jnp.where(top, zero, xt)], axis=1)


def _hgrn2_pair(q, k, v, gs, lf, st, tri2, lvl2, gn2):
    n_lv = len(HG_LEVEL_HALVES)
    w2 = 2 * LANES
    lane = lax.broadcasted_iota(jnp.int32, (BLK, w2), 1)
    first = lane < LANES
    lane_s = lax.broadcasted_iota(jnp.int32, (w2, w2), 1)
    row_s = lax.broadcasted_iota(jnp.int32, (w2, w2), 0)
    diag_blocks = (lane_s < LANES) == (row_s < LANES)
    hi = lf.astype(BF16)
    lo = (lf - hi.astype(F32)).astype(BF16)
    b = _dot(tri2, jnp.concatenate([hi, lo], axis=0))
    o = _dot_nt((q * jnp.exp(b)).astype(BF16), st.astype(BF16))
    att = jnp.zeros((BLK, w2), F32)
    for li, h in enumerate(HG_LEVEL_HALVES):
        e = jnp.exp(_level_exponents(b, h))
        a = _dot((q * e).astype(BF16), _block_diag_t(k * e))
        att = jnp.where(lvl2 == li + 1, a, att)
    b3 = b.reshape(BLK // HG_DIAG, HG_DIAG, w2)
    xd = (b3 - b3[:, HG_DIAG // 2 - 1:HG_DIAG // 2, :]).reshape(BLK, w2)
    a = _dot((q * jnp.exp(xd)).astype(BF16), _block_diag_t(k * jnp.exp(-xd)))
    att = jnp.where(lvl2 == n_lv + 1, a, att)
    o = o + _dot(att.astype(BF16), _block_diag(v, first))
    b_last = b[BLK - 1:BLK]
    k_end = (k * jnp.exp(b_last - b)).astype(BF16)
    vt = v.T
    st_new = jnp.where(diag_blocks, st * jnp.exp(b_last) + _dot(vt, k_end), 0.0)
    outs = [_rms(o[:, t * LANES:(t + 1) * LANES]) for t in range(2)]
    return jnp.concatenate(outs, axis=1) * gn2 * gs, st_new


def _mix_kernel(sink_ref, x_ref, uv_ref, qkv_ref, hg_ref, lf_ref, gate_ref, g1_ref, gpost_ref,
                ws_ref, bs_ref, tri_ref, lvl_ref, gn_ref, pa_ref, pb_ref, pc_ref, wo_ref,
                o_ref, kprev_ref, vprev_ref, state_ref, *, layer):
    step = pl.program_id(1)

    @pl.when(step == 0)
    def _():
        kprev_ref[...] = jnp.zeros_like(kprev_ref)
        vprev_ref[...] = jnp.zeros_like(vprev_ref)
        state_ref[...] = jnp.zeros_like(state_ref)

    row = lax.broadcasted_iota(jnp.int32, (BLK, BLK), 0)
    col = lax.broadcasted_iota(jnp.int32, (BLK, BLK), 1)
    ws_bf = [jnp.where(col <= row, ws_ref[g], 0.0).astype(BF16) for g in range(GM_GROUPS)]
    tri2 = tri_ref[...]
    lvl2 = lvl_ref[...]
    gn2 = gn_ref[...]
    k_prev = [kprev_ref[j] for j in range(N_KV_HEADS)]
    v_prev = [vprev_ref[j] for j in range(N_KV_HEADS)]
    states = [state_ref[p] for p in range(HG_HEADS // 2)]
    ya, yb, yc = [], [], []
    for sb in range(MIX_SUB):
        rows = slice(sb * BLK, (sb + 1) * BLK)
        ya.append(_spatial_gating(uv_ref[0, rows, :], ws_bf, bs_ref))
        first_key = jnp.where(step > 0, 0, BLK) if sb == 0 else 0
        o_b, k_prev, v_prev = _window_attention(qkv_ref[0, rows, :], k_prev, v_prev, sink_ref,
                                                layer * N_Q_HEADS, first_key)
        yb.append(o_b)
        hg = hg_ref[0, rows, :]
        lf = lf_ref[0, rows, :]
        heads = []
        for p in range(HG_HEADS // 2):
            def part(base):
                return hg[:, base + p * 2 * LANES:base + (p + 1) * 2 * LANES]
            o_c, states[p] = _hgrn2_pair(
                part(0).astype(F32), part(512).astype(F32), part(1024), part(1536).astype(F32),
                lf[:, p * 2 * LANES:(p + 1) * 2 * LANES], states[p], tri2, lvl2, gn2)
            heads.append(o_c)
        yc.append(jnp.concatenate(heads, axis=1))
    for j in range(N_KV_HEADS):
        kprev_ref[j] = k_prev[j]
        vprev_ref[j] = v_prev[j]
    for p in range(HG_HEADS // 2):
        state_ref[p] = states[p]
    y_a = jnp.concatenate(ya, axis=0).astype(BF16)
    y_b = jnp.concatenate(yb, axis=0).astype(BF16)
    y_c = jnp.concatenate(yc, axis=0).astype(BF16)
    gates = gate_ref[0]
    merged = (gates[:, 0:D_MODEL].astype(F32) * _dot(y_a, pa_ref[:, 0:D_MODEL])
              + gates[:, D_MODEL:2 * D_MODEL].astype(F32) * _dot(y_b, pb_ref[:, 0:D_MODEL])
              + gates[:, 2 * D_MODEL:3 * D_MODEL].astype(F32) * _dot(y_c, pc_ref[:, 0:D_MODEL]))
    y = _dot(merged.astype(BF16), wo_ref[:, 0:D_MODEL])
    o_ref[0] = x_ref[0] + g1_ref[0] * (_rms(y) * gpost_ref[...])


def _mix_call(layer, sinks, x, uv, qkv, hg, lf, gates, mod, gpost, ws, bs_b, tri, lvl, gn, pa, pb, pc, wo):
    nb, s, _ = x.shape
    tb = MIX_SUB * BLK

    def tok(width):
        return pl.BlockSpec((1, tb, width), lambda b, i: (b, i, 0))

    return pl.pallas_call(
        functools.partial(_mix_kernel, layer=layer),
        grid=(nb, s // tb),
        in_specs=[pl.BlockSpec(memory_space=pltpu.SMEM),
                  tok(D_MODEL), tok(1024), tok(1024), tok(2048), tok(512), tok(3072),
                  _mod_spec(layer, nb, 2), _layer_spec(gpost, layer),
                  _layer_spec(ws, layer), _layer_spec(bs_b, layer),
                  _const_spec((BLK, 2 * BLK)), _const_spec((BLK, 2 * BLK)), _layer_spec(gn, layer),
                  _layer_spec(pa, layer), _layer_spec(pb, layer), _layer_spec(pc, layer), _layer_spec(wo, layer)],
        out_specs=tok(D_MODEL),
        out_shape=jax.ShapeDtypeStruct(x.shape, F32),
        scratch_shapes=[pltpu.VMEM((N_KV_HEADS, BLK, LANES), BF16),
                        pltpu.VMEM((N_KV_HEADS, BLK, LANES), BF16),
                        pltpu.VMEM((HG_HEADS // 2, 2 * LANES, 2 * HG_EXPAND), F32)],
        compiler_params=pltpu.CompilerParams(
            dimension_semantics=("parallel", "arbitrary"), vmem_limit_bytes=VMEM_LIMIT),
        name="mixer_core",
    )(sinks, x, uv, qkv, hg, lf, gates, mod, gpost, ws, bs_b, tri, lvl, gn, pa, pb, pc, wo)


def _ffn_kernel(x_ref, sh_ref, sc_ref, gate_ref, gpre_ref, gpost_ref, w1_ref, cw_ref, cb_ref, w2_ref,
                o_ref, a_ref, hid_ref):
    step = pl.program_id(1)
    tm = x_ref.shape[1]

    @pl.when(step == 0)
    def _():
        a_ref[:, 0:8, :] = jnp.zeros((a_ref.shape[0], 8, LANES), F32)

    x = x_ref[0]
    h = _rms(x) * gpre_ref[...]
    hb = (h * (1.0 + sc_ref[0]) + sh_ref[0]).astype(BF16)

    def conv(col):
        a = _dot(hb, w1_ref[:, col:col + FF_CHUNK])
        taps = []
        for t in range(FF_CHUNK // LANES):
            a_ref[col // LANES + t, 8:8 + tm, :] = a[:, t * LANES:(t + 1) * LANES]
        for t in range(FF_CHUNK // LANES):
            slab = col // LANES + t
            sl = slice(col + t * LANES, col + (t + 1) * LANES)
            cw = cw_ref[:, sl]
            taps.append(cw[0:1] * a_ref[slab, 6:6 + tm, :] + cw[1:2] * a_ref[slab, 7:7 + tm, :]
                        + cw[2:3] * a[:, t * LANES:(t + 1) * LANES] + cb_ref[:, sl])
            a_ref[slab, 0:8, :] = a[tm - 8:tm, t * LANES:(t + 1) * LANES]
        return jnp.concatenate(taps, axis=1)

    for c in range(D_FF // FF_CHUNK):
        gate = conv(c * FF_CHUNK)
        val = conv(D_FF + c * FF_CHUNK)
        hid_ref[:, c * FF_CHUNK:(c + 1) * FF_CHUNK] = (_silu(gate) * val).astype(BF16)
    y = _dot(hid_ref[...], w2_ref[:, 0:D_MODEL])
    o_ref[0] = x + gate_ref[0] * (_rms(y) * gpost_ref[...])


def _ffn_call(layer, x, mod, gpre, gpost, w1, cw, cb, w2):
    nb, s, _ = x.shape
    tm = TM_FFN
    tok = pl.BlockSpec((1, tm, D_MODEL), lambda b, i: (b, i, 0))
    return pl.pallas_call(
        _ffn_kernel,
        grid=(nb, s // tm),
        in_specs=[tok, _mod_spec(layer, nb, 3), _mod_spec(layer, nb, 4), _mod_spec(layer, nb, 5),
                  _layer_spec(gpre, layer), _layer_spec(gpost, layer),
                  _layer_spec(w1, layer), _layer_spec(cw, layer), _layer_spec(cb, layer), _layer_spec(w2, layer)],
        out_specs=tok,
        out_shape=jax.ShapeDtypeStruct(x.shape, F32),
        scratch_shapes=[pltpu.VMEM((2 * D_FF // LANES, 8 + tm, LANES), F32), pltpu.VMEM((tm, D_FF), BF16)],
        compiler_params=pltpu.CompilerParams(
            dimension_semantics=("parallel", "arbitrary"), vmem_limit_bytes=VMEM_LIMIT),
        name="conv_ffn",
    )(x, mod, mod, mod, gpre, gpost, w1, cw, cb, w2)


def _w_in_pieces():
    k0 = 2 * GM_WIDTH + ATT_WIDTH
    pieces = [(0, 0, k0)]
    dst = k0
    for j in range(2 * N_KV_HEADS):
        for _ in range(2):
            pieces.append((k0 + j * HEAD_DIM, dst, HEAD_DIM))
            dst += HEAD_DIM
    src = k0 + 2 * N_KV_HEADS * HEAD_DIM
    pieces.append((src, dst, C_END - dst))
    return tuple(pieces)


def kernel(x, c, positions, ada_w, ada_b, norm_mix_pre, norm_mix_post, norm_ffn_pre, norm_ffn_post, w_in,
           gm_ln_g, gm_ln_b, gm_ws, gm_bs, attn_sinks, hg_lb_logits, hg_gnorm, proj_a, proj_b, proj_c, w_out,
           ffn_w1, ffn_conv_w, ffn_conv_b, ffn_w2):
    mod = _ada_mod(c, ada_w, ada_b).reshape(-1, 1, D_MODEL)
    lb_all = _rows(_forget_lower_bounds(hg_lb_logits))
    cos_t, sin_t = _rope_tables(positions)
    tri2 = jnp.asarray(np.tile(_HG_TRI_NP, (1, 2)), dtype=BF16)
    lvl2 = jnp.asarray(np.tile(_HG_LVL_NP, (1, 2)))
    w_in_b = _mxu_weight(w_in, _w_in_pieces())
    pa, pb, pc, wo = (_mxu_weight(t) for t in (proj_a, proj_b, proj_c, w_out))
    w1, w2 = _mxu_weight(ffn_w1), _mxu_weight(ffn_w2)
    bs_b = jnp.broadcast_to(gm_bs[:, :, :, None], (DEPTH, GM_GROUPS, BLK, BLK))
    sinks = attn_sinks.reshape(-1)
    gn2 = _rows(jnp.tile(hg_gnorm, (1, 2)))
    g_mix_pre, g_mix_post = _rows(norm_mix_pre), _rows(norm_mix_post)
    g_ffn_pre, g_ffn_post = _rows(norm_ffn_pre), _rows(norm_ffn_post)
    ln_g, ln_b, conv_b = _rows(gm_ln_g), _rows(gm_ln_b), _rows(ffn_conv_b)

    for l in range(DEPTH):
        uv, qkv, hg, lf, gates = _in_call(l, x, mod, g_mix_pre, w_in_b, cos_t, sin_t, lb_all, ln_g, ln_b)
        x = _mix_call(l, sinks, x, uv, qkv, hg, lf, gates, mod, g_mix_post, gm_ws, bs_b, tri2, lvl2, gn2,
                      pa, pb, pc, wo)
        x = _ffn_call(l, x, mod, g_ffn_pre, g_ffn_post, w1, ffn_conv_w, conv_b, w2)
    return x
```

```python
import functools

import jax
import jax.numpy as jnp
import numpy as np
from jax import lax
from jax.experimental import pallas as pl
from jax.experimental.pallas import tpu as pltpu

F32 = jnp.float32
BF16 = jnp.bfloat16

D_MODEL = 1024
DEPTH = 4
GM_WIDTH = 512
GM_GROUPS = 4
N_Q_HEADS = 8
N_KV_HEADS = 2
HEAD_DIM = 64
ATT_WIDTH = N_Q_HEADS * HEAD_DIM
ROPE_THETA = 10000.0
MASK_VALUE = -1e30
HG_HEADS = 4
HG_EXPAND = 128
HG_KEY_WIDTH = HG_HEADS * HG_EXPAND
HG_WIDTH = HG_HEADS * 128
HG_MIN_FORGET = 1e-6
D_FF = 2816
EPS = 1e-6

LANES = 128
BLK = 128
MIX_SUB = 4
TM_IN = 256
TM_FFN = 512
FF_CHUNK = 256
VMEM_LIMIT = 56 * 1024 * 1024
WEIGHT_PREP_BLOCK_BYTES = 8 * 1024 * 1024

C_GU, C_GV, C_Q, C_K, C_V, C_HQ, C_HF, C_HI, C_HG, C_GATE, C_END = (
    0, 512, 1024, 1536, 1664, 1792, 2304, 2816, 3328, 3840, 6912)

HG_LEVEL_HALVES = (64, 32, 16, 8)
HG_DIAG = 8


def _hgrn_tables():
    t = np.arange(BLK)
    tri = (t[None, :] <= t[:, None]).astype(np.float32)
    lvl = np.zeros((BLK, BLK), np.int32)
    for li, h in enumerate(HG_LEVEL_HALVES):
        blk = 2 * h
        second = (t % blk) >= h
        same = (t[:, None] // blk) == (t[None, :] // blk)
        lvl[same & second[:, None] & (~second)[None, :]] = li + 1
    same = (t[:, None] // HG_DIAG) == (t[None, :] // HG_DIAG)
    lvl[same & (t[None, :] <= t[:, None])] = len(HG_LEVEL_HALVES) + 1
    return tri, lvl


_HG_TRI_NP, _HG_LVL_NP = _hgrn_tables()


def _dot(a, b):
    return jnp.dot(a, b, preferred_element_type=F32)


def _dot_nt(a, b):
    return lax.dot_general(a, b, (((1,), (1,)), ((), ())), preferred_element_type=F32)


def _sigmoid(x):
    return 1.0 / (1.0 + jnp.exp(-x))


def _silu(x):
    return x * _sigmoid(x)


def _gelu_tanh(x):
    return 0.5 * x * (1.0 + jnp.tanh(np.sqrt(2.0 / np.pi).astype(np.float32) * (x + 0.044715 * (x * x * x))))


def _rms(x):
    return x * lax.rsqrt(jnp.mean(x * x, axis=-1, keepdims=True) + EPS)


def _const_spec(shape):
    nd = len(shape)
    return pl.BlockSpec(shape, lambda b, i: (0,) * nd, pipeline_mode=pl.Buffered(1))


def _layer_spec(arr, layer):
    shape = arr.shape[1:]
    return pl.BlockSpec((None,) + shape, lambda b, i: (layer,) + (0,) * len(shape), pipeline_mode=pl.Buffered(1))


def _mod_spec(layer, nb, part):
    return pl.BlockSpec((1, 1, D_MODEL), lambda b, i: ((layer * nb + b) * 6 + part, 0, 0))


def _weight_kernel(w_ref, o_ref):
    n = w_ref.shape[1]
    o_ref[:, 0:n] = w_ref[...].astype(BF16)
    if n < o_ref.shape[1]:
        o_ref[:, n:] = jnp.zeros((o_ref.shape[0], o_ref.shape[1] - n), BF16)


def _mxu_weight(w):
    depth, k, n = w.shape
    width = n + LANES if (n // LANES) % 2 == 0 else n
    tk = max(t for t in (256, 512, 1024, 1408) if k % t == 0 and t * n * 4 <= WEIGHT_PREP_BLOCK_BYTES)
    return pl.pallas_call(
        _weight_kernel,
        grid=(depth, k // tk),
        in_specs=[pl.BlockSpec((None, tk, n), lambda l, i: (l, i, 0))],
        out_specs=pl.BlockSpec((None, tk, width), lambda l, i: (l, i, 0)),
        out_shape=jax.ShapeDtypeStruct((depth, k, width), BF16),
        compiler_params=pltpu.CompilerParams(
            dimension_semantics=("parallel", "parallel"), vmem_limit_bytes=VMEM_LIMIT),
        name="weight_prep",
    )(w)


def _rows(v):
    return v.reshape(v.shape[0], 1, -1)


def _ada_kernel(c_ref, w_ref, b_ref, o_ref):
    c = c_ref[...]
    o_ref[0] = _dot(_silu(c).astype(BF16), w_ref[0].astype(BF16)) + b_ref[0]


def _ada_mod(c, ada_w, ada_b):
    nb = c.shape[0]
    ncol = ada_w.shape[2] // D_MODEL
    return pl.pallas_call(
        _ada_kernel,
        grid=(DEPTH, ncol),
        in_specs=[pl.BlockSpec((nb, D_MODEL), lambda l, j: (0, 0)),
                  pl.BlockSpec((1, D_MODEL, D_MODEL), lambda l, j: (l, 0, j)),
                  pl.BlockSpec((1, 1, D_MODEL), lambda l, j: (l, 0, j))],
        out_specs=pl.BlockSpec((1, nb, D_MODEL), lambda l, j: (l, 0, j)),
        out_shape=jax.ShapeDtypeStruct((DEPTH, nb, ncol * D_MODEL), F32),
        name="ada_mod",
    )(c, ada_w, ada_b.reshape(DEPTH, 1, -1))


def _lb_kernel(x_ref, o_ref):
    x = x_ref[...]
    e = jnp.exp(x - jnp.max(x, axis=0, keepdims=True))
    p = e / jnp.sum(e, axis=0, keepdims=True)
    run = jnp.zeros_like(p[0:1])
    rows = []
    for l in range(DEPTH):
        run = run + p[l:l + 1]
        rows.append(run - p[0:1])
    o_ref[...] = jnp.concatenate(rows, axis=0)


def _forget_lower_bounds(logits):
    return pl.pallas_call(
        _lb_kernel, out_shape=jax.ShapeDtypeStruct(logits.shape, F32), name="hgrn_lower_bounds",
    )(logits.astype(F32))


def _rope_kernel(pos_ref, inv_ref, sign_ref, cos_ref, sin_ref):
    ang = pos_ref[0] * inv_ref[...]
    cos_ref[0] = jnp.cos(ang)
    sin_ref[0] = jnp.sin(ang) * sign_ref[...]


def _rope_tables(positions):
    nb, s = positions.shape
    half = HEAD_DIM // 2
    inv = ROPE_THETA ** (-jnp.arange(half, dtype=F32) / half)
    inv_t = jnp.tile(inv, LANES // half)[None, :]
    sign = np.where((np.arange(LANES) % HEAD_DIM) < half, -1.0, 1.0).astype(np.float32)[None, :]
    ts = 512
    tab = jax.ShapeDtypeStruct((nb, s, LANES), F32)
    return pl.pallas_call(
        _rope_kernel,
        grid=(nb, s // ts),
        in_specs=[pl.BlockSpec((1, ts, 1), lambda b, i: (b, i, 0)),
                  pl.BlockSpec((1, LANES), lambda b, i: (0, 0)),
                  pl.BlockSpec((1, LANES), lambda b, i: (0, 0))],
        out_specs=[pl.BlockSpec((1, ts, LANES), lambda b, i: (b, i, 0))] * 2,
        out_shape=[tab, tab],
        name="rope_tables",
    )(positions.astype(F32)[..., None], inv_t, jnp.asarray(sign))


def _rope_apply(x, cos, sin_signed, first_half):
    partner = jnp.where(first_half, pltpu.roll(x, LANES - HEAD_DIM // 2, 1), pltpu.roll(x, HEAD_DIM // 2, 1))
    return x * cos + partner * sin_signed


def _in_kernel(x_ref, sh_ref, sc_ref, g_ref, w_ref, cos_ref, sin_ref, lb_ref, lng_ref, lnb_ref,
               uv_ref, qkv_ref, hg_ref, lf_ref, gate_ref):
    x = x_ref[0]
    h = _rms(x) * g_ref[...]
    h = h * (1.0 + sc_ref[0]) + sh_ref[0]
    hb = h.astype(BF16)

    def proj(a, b):
        return _dot(hb, w_ref[:, a:b])

    cos = cos_ref[0]
    sin = sin_ref[0]
    lane = lax.broadcasted_iota(jnp.int32, cos.shape, 1)
    first_half = (lane % HEAD_DIM) < (HEAD_DIM // 2)
    lb = lb_ref[...]

    def epi_u(d):
        uv_ref[0, :, 0:GM_WIDTH] = _gelu_tanh(d).astype(BF16)

    def epi_v(d):
        v = _gelu_tanh(d)
        mu = jnp.mean(v, axis=-1, keepdims=True)
        vc = v - mu
        var = jnp.mean(vc * vc, axis=-1, keepdims=True)
        uv_ref[0, :, GM_WIDTH:2 * GM_WIDTH] = (vc * lax.rsqrt(var + EPS) * lng_ref[...] + lnb_ref[...]).astype(BF16)

    def epi_q(d):
        for gidx in range(d.shape[1] // LANES):
            r = _rope_apply(d[:, gidx * LANES:(gidx + 1) * LANES], cos, sin, first_half) * (HEAD_DIM ** -0.5)
            qkv_ref[0, :, gidx * LANES:(gidx + 1) * LANES] = r.astype(BF16)

    def epi_kv(d):
        low = (lane % LANES) < HEAD_DIM
        for t, x in enumerate((_rope_apply(d[:, 0:LANES], cos, sin, first_half), d[:, LANES:2 * LANES])):
            swapped = pltpu.roll(x, HEAD_DIM, 1)
            base = ATT_WIDTH + t * 2 * LANES
            qkv_ref[0, :, base:base + LANES] = jnp.where(low, x, swapped).astype(BF16)
            qkv_ref[0, :, base + LANES:base + 2 * LANES] = jnp.where(low, swapped, x).astype(BF16)

    def epi_hq(d):
        hg_ref[0, :, 0:512] = (_silu(d) * (HG_EXPAND ** -0.5)).astype(BF16)

    def epi_hf(fl):
        forget = lb + (1.0 - lb) * _sigmoid(fl)
        lf_ref[0] = jnp.log(jnp.maximum(forget, HG_MIN_FORGET))
        hg_ref[0, :, 512:1024] = ((1.0 - lb) * _sigmoid(-fl)).astype(BF16)

    def epi_hi(d):
        hg_ref[0, :, 1024:1536] = d.astype(BF16)

    def epi_hgate(d):
        hg_ref[0, :, 1536:2048] = _silu(d).astype(BF16)

    def epi_gate(gidx):
        def f(d):
            gate_ref[0, :, gidx * 512:(gidx + 1) * 512] = _sigmoid(d).astype(BF16)
        return f

    def gate_stage(g):
        return (C_GATE + g * 512, C_GATE + (g + 1) * 512, epi_gate(g))

    stages = [(C_GU, C_GV, epi_u), gate_stage(0), (C_GV, C_Q, epi_v), gate_stage(1), gate_stage(2),
              (C_Q, C_K, epi_q), gate_stage(3), (C_HQ, C_HF, epi_hq), gate_stage(4), (C_HF, C_HI, epi_hf),
              gate_stage(5), (C_HG, C_GATE, epi_hgate), (C_K, C_HQ, epi_kv), (C_HI, C_HG, epi_hi)]
    for a, b, epi in stages:
        epi(proj(a, b))


def _in_call(layer, x, mod, g, w, cos_t, sin_t, lb, lng, lnb):
    nb, s, _ = x.shape
    tm = TM_IN

    def tok(width):
        return pl.BlockSpec((1, tm, width), lambda b, i: (b, i, 0))

    def out(width, dt):
        return jax.ShapeDtypeStruct((nb, s, width), dt)

    return pl.pallas_call(
        _in_kernel,
        grid=(nb, s // tm),
        in_specs=[tok(D_MODEL), _mod_spec(layer, nb, 0), _mod_spec(layer, nb, 1), _layer_spec(g, layer),
                  _layer_spec(w, layer), tok(LANES), tok(LANES),
                  _layer_spec(lb, layer), _layer_spec(lng, layer), _layer_spec(lnb, layer)],
        out_specs=[tok(1024), tok(1024), tok(2048), tok(512), tok(3072)],
        out_shape=[out(1024, BF16), out(1024, BF16), out(2048, BF16), out(512, F32), out(3072, BF16)],
        compiler_params=pltpu.CompilerParams(
            dimension_semantics=("parallel", "parallel"), vmem_limit_bytes=VMEM_LIMIT),
        name="mixer_in",
    )(x, mod, mod, g, w, cos_t, sin_t, lb, lng, lnb)


def _spatial_gating(uv, ws_bf, bs_ref):
    outs = []
    for g in range(GM_GROUPS):
        sl = slice(g * LANES, (g + 1) * LANES)
        s = _dot(ws_bf[g], uv[:, GM_WIDTH + g * LANES:GM_WIDTH + (g + 1) * LANES]) + bs_ref[g]
        outs.append(uv[:, sl].astype(F32) * s)
    return jnp.concatenate(outs, axis=1)


def _interleave(chains):
    live = list(chains)
    rnd = 0
    while live:
        for item in list(live):
            if item[0] <= rnd:
                try:
                    next(item[1])
                except StopIteration:
                    live.remove(item)
        rnd += 1


def _attention_kv_head(j, qkv, k_prev, v_prev, sink_ref, sink_base, first_key, out):
    nk = 2 * BLK
    qi = lax.broadcasted_iota(jnp.int32, (BLK, nk), 0)
    kj = lax.broadcasted_iota(jnp.int32, (BLK, nk), 1)
    valid = (kj > qi) & (kj <= qi + BLK) & (kj >= first_key)
    row_t = lax.broadcasted_iota(jnp.int32, (LANES, nk), 0)
    low_t = row_t < HEAD_DIM
    lane_o = lax.broadcasted_iota(jnp.int32, (BLK, LANES), 1)
    low_o = lane_o < HEAD_DIM
    pairs_per_kv = (N_Q_HEADS // N_KV_HEADS) // 2
    k_cur = qkv[:, ATT_WIDTH + j * LANES:ATT_WIDTH + (j + 1) * LANES]
    v_cur = qkv[:, ATT_WIDTH + (N_KV_HEADS + j) * LANES:ATT_WIDTH + (N_KV_HEADS + j + 1) * LANES]
    kt = jnp.concatenate([k_prev, k_cur], axis=0).T
    zero = jnp.zeros_like(kt)
    k_sel = jnp.concatenate([jnp.where(low_t, kt, zero), jnp.where(low_t, zero, kt)], axis=1)
    vd = jnp.concatenate([jnp.concatenate([v_prev, v_cur], axis=0), jnp.ones((nk, LANES), BF16)], axis=1)
    q4 = jnp.concatenate([qkv[:, (j * pairs_per_kv + pp) * LANES:(j * pairs_per_kv + pp + 1) * LANES]
                          for pp in range(pairs_per_kv)], axis=0)
    s_all = _dot(q4, k_sel)
    yield
    probs, sink_terms = [], []
    for pp in range(pairs_per_kv):
        for par in range(2):
            sink = sink_ref[sink_base + 2 * (j * pairs_per_kv + pp) + par]
            s = jnp.where(valid, s_all[pp * BLK:(pp + 1) * BLK, par * nk:(par + 1) * nk], MASK_VALUE)
            m = jnp.maximum(jnp.max(s, axis=-1, keepdims=True), sink)
            probs.append(jnp.exp(s - m).astype(BF16))
            sink_terms.append(jnp.exp(sink - m))
    o_all = _dot(jnp.concatenate(probs, axis=0), vd)
    yield
    outs = []
    for pp in range(pairs_per_kv):
        halves = []
        for par in range(2):
            blk = o_all[(2 * pp + par) * BLK:(2 * pp + par + 1) * BLK]
            halves.append(blk[:, :LANES] / (blk[:, LANES:] + sink_terms[2 * pp + par]))
        outs.append(jnp.where(low_o, halves[0], halves[1]))
    out["y"] = jnp.concatenate(outs, axis=1)


def _level_exponents(b, h):
    pieces = []
    for k in range(BLK // (2 * h)):
        blk = b[k * 2 * h:(k + 1) * 2 * h]
        r = blk[h - 1:h]
        pieces.append(r - blk[:h])
        pieces.append(blk[h:] - r)
    return jnp.concatenate(pieces, axis=0)


def _block_diag(x2, top_mask):
    zero = jnp.zeros_like(x2)
    return jnp.concatenate([jnp.where(top_mask, x2, zero), jnp.where(top_mask, zero, x2)], axis=0)


def _block_diag_t(x2):
    xt = x2.astype(BF16).T
    row = lax.broadcasted_iota(jnp.int32, xt.shape, 0)
    top = row < LANES
    zero = jnp.zeros_like(xt)
    return jnp.concatenate([jnp.where(top, xt, zero), jnp.where(top, zero, xt)], axis=1)


def _hgrn2_pair(q, k, v, gs, lf, prev, tri2, lvl2, gn2, out):
    n_lv = len(HG_LEVEL_HALVES)
    w2 = 2 * LANES
    lane = lax.broadcasted_iota(jnp.int32, (BLK, w2), 1)
    first = lane < LANES
    lane_s = lax.broadcasted_iota(jnp.int32, (w2, w2), 1)
    row_s = lax.broadcasted_iota(jnp.int32, (w2, w2), 0)
    diag_blocks = (lane_s < LANES) == (row_s < LANES)
    hi = lf.astype(BF16)
    lo = (lf - hi.astype(F32)).astype(BF16)
    b = _dot(tri2, jnp.concatenate([hi, lo], axis=0))
    yield
    st = prev["st"]
    o = _dot_nt((q * jnp.exp(b)).astype(BF16), st.astype(BF16))
    att = jnp.zeros((BLK, w2), F32)
    for li, h in enumerate(HG_LEVEL_HALVES):
        e = jnp.exp(_level_exponents(b, h))
        a = _dot((q * e).astype(BF16), _block_diag_t(k * e))
        att = jnp.where(lvl2 == li + 1, a, att)
    b3 = b.reshape(BLK // HG_DIAG, HG_DIAG, w2)
    xd = (b3 - b3[:, HG_DIAG // 2 - 1:HG_DIAG // 2, :]).reshape(BLK, w2)
    a = _dot((q * jnp.exp(xd)).astype(BF16), _block_diag_t(k * jnp.exp(-xd)))
    att = jnp.where(lvl2 == n_lv + 1, a, att)
    b_last = b[BLK - 1:BLK]
    k_end = (k * jnp.exp(b_last - b)).astype(BF16)
    yield
    o = o + _dot(att.astype(BF16), _block_diag(v, first))
    out["st"] = jnp.where(diag_blocks, st * jnp.exp(b_last) + _dot(v.T, k_end), 0.0)
    yield
    outs = [_rms(o[:, t * LANES:(t + 1) * LANES]) for t in range(2)]
    out["y"] = jnp.concatenate(outs, axis=1) * gn2 * gs


def _mix_kernel(sink_ref, x_ref, uv_ref, qkv_ref, hg_ref, lf_ref, gate_ref, g1_ref, gpost_ref,
                ws_ref, bs_ref, tri_ref, lvl_ref, gn_ref, pa_ref, pb_ref, pc_ref, wo_ref,
                o_ref, kprev_ref, vprev_ref, state_ref, *, layer):
    step = pl.program_id(1)

    @pl.when(step == 0)
    def _():
        kprev_ref[...] = jnp.zeros_like(kprev_ref)
        vprev_ref[...] = jnp.zeros_like(vprev_ref)
        state_ref[...] = jnp.zeros_like(state_ref)

    row = lax.broadcasted_iota(jnp.int32, (BLK, BLK), 0)
    col = lax.broadcasted_iota(jnp.int32, (BLK, BLK), 1)
    ws_bf = [jnp.where(col <= row, ws_ref[g], 0.0).astype(BF16) for g in range(GM_GROUPS)]
    tri2 = tri_ref[...]
    lvl2 = lvl_ref[...]
    gn2 = gn_ref[...]
    n_pair = HG_HEADS // 2
    k_off = ATT_WIDTH
    v_off = ATT_WIDTH + N_KV_HEADS * LANES
    att_out = [[dict() for _ in range(N_KV_HEADS)] for _ in range(MIX_SUB)]
    hg_out = [[dict() for _ in range(n_pair)] for _ in range(MIX_SUB)]
    hg_init = [{"st": state_ref[p]} for p in range(n_pair)]
    chains = []
    for sb in range(MIX_SUB):
        rows = slice(sb * BLK, (sb + 1) * BLK)
        first_key = jnp.where(step > 0, 0, BLK) if sb == 0 else 0
        qkv = qkv_ref[0, rows, :]
        hg = hg_ref[0, rows, :]
        lf = lf_ref[0, rows, :]
        for p in range(n_pair):
            def part(base):
                return hg[:, base + p * 2 * LANES:base + (p + 1) * 2 * LANES]
            prev = hg_init[p] if sb == 0 else hg_out[sb - 1][p]
            chains.append((sb, _hgrn2_pair(
                part(0).astype(F32), part(512).astype(F32), part(1024), part(1536).astype(F32),
                lf[:, p * 2 * LANES:(p + 1) * 2 * LANES], prev, tri2, lvl2, gn2, hg_out[sb][p])))
        for j in range(N_KV_HEADS):
            if sb == 0:
                kp, vp = kprev_ref[j], vprev_ref[j]
            else:
                prow = slice((sb - 1) * BLK, sb * BLK)
                kp = qkv_ref[0, prow, k_off + j * LANES:k_off + (j + 1) * LANES]
                vp = qkv_ref[0, prow, v_off + j * LANES:v_off + (j + 1) * LANES]
            chains.append((sb, _attention_kv_head(j, qkv, kp, vp, sink_ref, layer * N_Q_HEADS, first_key,
                                                  att_out[sb][j])))
    _interleave(chains)
    ya = [_spatial_gating(uv_ref[0, sb * BLK:(sb + 1) * BLK, :], ws_bf, bs_ref) for sb in range(MIX_SUB)]
    yb = [jnp.concatenate([att_out[sb][j]["y"] for j in range(N_KV_HEADS)], axis=1) for sb in range(MIX_SUB)]
    yc = [jnp.concatenate([hg_out[sb][p]["y"] for p in range(n_pair)], axis=1) for sb in range(MIX_SUB)]
    last = slice((MIX_SUB - 1) * BLK, MIX_SUB * BLK)
    for j in range(N_KV_HEADS):
        kprev_ref[j] = qkv_ref[0, last, k_off + j * LANES:k_off + (j + 1) * LANES]
        vprev_ref[j] = qkv_ref[0, last, v_off + j * LANES:v_off + (j + 1) * LANES]
    for p in range(n_pair):
        state_ref[p] = hg_out[MIX_SUB - 1][p]["st"]
    y_a = jnp.concatenate(ya, axis=0).astype(BF16)
    y_b = jnp.concatenate(yb, axis=0).astype(BF16)
    y_c = jnp.concatenate(yc, axis=0).astype(BF16)
    gates = gate_ref[0]
    merged = (gates[:, 0:D_MODEL].astype(F32) * _dot(y_a, pa_ref[:, 0:D_MODEL])
              + gates[:, D_MODEL:2 * D_MODEL].astype(F32) * _dot(y_b, pb_ref[:, 0:D_MODEL])
              + gates[:, 2 * D_MODEL:3 * D_MODEL].astype(F32) * _dot(y_c, pc_ref[:, 0:D_MODEL]))
    y = _dot(merged.astype(BF16), wo_ref[:, 0:D_MODEL])
    o_ref[0] = x_ref[0] + g1_ref[0] * (_rms(y) * gpost_ref[...])


def _mix_call(layer, sinks, x, uv, qkv, hg, lf, gates, mod, gpost, ws, bs_b, tri, lvl, gn, pa, pb, pc, wo):
    nb, s, _ = x.shape
    tb = MIX_SUB * BLK

    def tok(width):
        return pl.BlockSpec((1, tb, width), lambda b, i: (b, i, 0))

    return pl.pallas_call(
        functools.partial(_mix_kernel, layer=layer),
        grid=(nb, s // tb),
        in_specs=[pl.BlockSpec(memory_space=pltpu.SMEM),
                  tok(D_MODEL), tok(1024), tok(1024), tok(2048), tok(512), tok(3072),
                  _mod_spec(layer, nb, 2), _layer_spec(gpost, layer),
                  _layer_spec(ws, layer), _layer_spec(bs_b, layer),
                  _const_spec((BLK, 2 * BLK)), _const_spec((BLK, 2 * BLK)), _layer_spec(gn, layer),
                  _layer_spec(pa, layer), _layer_spec(pb, layer), _layer_spec(pc, layer), _layer_spec(wo, layer)],
        out_specs=tok(D_MODEL),
        out_shape=jax.ShapeDtypeStruct(x.shape, F32),
        scratch_shapes=[pltpu.VMEM((N_KV_HEADS, BLK, LANES), BF16),
                        pltpu.VMEM((N_KV_HEADS, BLK, LANES), BF16),
                        pltpu.VMEM((HG_HEADS // 2, 2 * LANES, 2 * HG_EXPAND), F32)],
        compiler_params=pltpu.CompilerParams(
            dimension_semantics=("parallel", "arbitrary"), vmem_limit_bytes=VMEM_LIMIT),
        name="mixer_core",
    )(sinks, x, uv, qkv, hg, lf, gates, mod, gpost, ws, bs_b, tri, lvl, gn, pa, pb, pc, wo)


def _ffn_kernel(x_ref, sh_ref, sc_ref, gate_ref, gpre_ref, gpost_ref, w1_ref, cw_ref, cb_ref, w2_ref,
                o_ref, a_ref, hid_ref):
    step = pl.program_id(1)
    tm = x_ref.shape[1]

    @pl.when(step == 0)
    def _():
        a_ref[:, 0:8, :] = jnp.zeros((a_ref.shape[0], 8, LANES), F32)

    x = x_ref[0]
    h = _rms(x) * gpre_ref[...]
    hb = (h * (1.0 + sc_ref[0]) + sh_ref[0]).astype(BF16)

    def conv(col):
        a = _dot(hb, w1_ref[:, col:col + FF_CHUNK])
        taps = []
        for t in range(FF_CHUNK // LANES):
            a_ref[col // LANES + t, 8:8 + tm, :] = a[:, t * LANES:(t + 1) * LANES]
        for t in range(FF_CHUNK // LANES):
            slab = col // LANES + t
            sl = slice(col + t * LANES, col + (t + 1) * LANES)
            cw = cw_ref[:, sl]
            taps.append(cw[0:1] * a_ref[slab, 6:6 + tm, :] + cw[1:2] * a_ref[slab, 7:7 + tm, :]
                        + cw[2:3] * a[:, t * LANES:(t + 1) * LANES] + cb_ref[:, sl])
            a_ref[slab, 0:8, :] = a[tm - 8:tm, t * LANES:(t + 1) * LANES]
        return jnp.concatenate(taps, axis=1)

    for c in range(D_FF // FF_CHUNK):
        gate = conv(c * FF_CHUNK)
        val = conv(D_FF + c * FF_CHUNK)
        hid_ref[:, c * FF_CHUNK:(c + 1) * FF_CHUNK] = (_silu(gate) * val).astype(BF16)
    y = _dot(hid_ref[...], w2_ref[:, 0:D_MODEL])
    o_ref[0] = x + gate_ref[0] * (_rms(y) * gpost_ref[...])


def _ffn_call(layer, x, mod, gpre, gpost, w1, cw, cb, w2):
    nb, s, _ = x.shape
    tm = TM_FFN
    tok = pl.BlockSpec((1, tm, D_MODEL), lambda b, i: (b, i, 0))
    return pl.pallas_call(
        _ffn_kernel,
        grid=(nb, s // tm),
        in_specs=[tok, _mod_spec(layer, nb, 3), _mod_spec(layer, nb, 4), _mod_spec(layer, nb, 5),
                  _layer_spec(gpre, layer), _layer_spec(gpost, layer),
                  _layer_spec(w1, layer), _layer_spec(cw, layer), _layer_spec(cb, layer), _layer_spec(w2, layer)],
        out_specs=tok,
        out_shape=jax.ShapeDtypeStruct(x.shape, F32),
        scratch_shapes=[pltpu.VMEM((2 * D_FF // LANES, 8 + tm, LANES), F32), pltpu.VMEM((tm, D_FF), BF16)],
        compiler_params=pltpu.CompilerParams(
            dimension_semantics=("parallel", "arbitrary"), vmem_limit_bytes=VMEM_LIMIT),
        name="conv_ffn",
    )(x, mod, mod, mod, gpre, gpost, w1, cw, cb, w2)


def kernel(x, c, positions, ada_w, ada_b, norm_mix_pre, norm_mix_post, norm_ffn_pre, norm_ffn_post, w_in,
           gm_ln_g, gm_ln_b, gm_ws, gm_bs, attn_sinks, hg_lb_logits, hg_gnorm, proj_a, proj_b, proj_c, w_out,
           ffn_w1, ffn_conv_w, ffn_conv_b, ffn_w2):
    mod = _ada_mod(c, ada_w, ada_b).reshape(-1, 1, D_MODEL)
    lb_all = _rows(_forget_lower_bounds(hg_lb_logits))
    cos_t, sin_t = _rope_tables(positions)
    tri2 = jnp.asarray(np.tile(_HG_TRI_NP, (1, 2)), dtype=BF16)
    lvl2 = jnp.asarray(np.tile(_HG_LVL_NP, (1, 2)))
    w_in_b = _mxu_weight(w_in)
    pa, pb, pc, wo = (_mxu_weight(t) for t in (proj_a, proj_b, proj_c, w_out))
    w1, w2 = _mxu_weight(ffn_w1), _mxu_weight(ffn_w2)
    bs_b = jnp.broadcast_to(gm_bs[:, :, :, None], (DEPTH, GM_GROUPS, BLK, BLK))
    sinks = attn_sinks.reshape(-1)
    gn2 = _rows(jnp.tile(hg_gnorm, (1, 2)))
    g_mix_pre, g_mix_post = _rows(norm_mix_pre), _rows(norm_mix_post)
    g_ffn_pre, g_ffn_post = _rows(norm_ffn_pre), _rows(norm_ffn_post)
    ln_g, ln_b, conv_b = _rows(gm_ln_g), _rows(gm_ln_b), _rows(ffn_conv_b)

    for l in range(DEPTH):
        uv, qkv, hg, lf, gates = _in_call(l, x, mod, g_mix_pre, w_in_b, cos_t, sin_t, lb_all, ln_g, ln_b)
        x = _mix_call(l, sinks, x, uv, qkv, hg, lf, gates, mod, g_mix_post, gm_ws, bs_b, tri2, lvl2, gn2,
                      pa, pb, pc, wo)
        x = _ffn_call(l, x, mod, g_ffn_pre, g_ffn_post, w1, ffn_conv_w, conv_b, w2)
    return x
```

```python
import functools

import jax
import jax.numpy as jnp
import numpy as np
from jax import lax
from jax.experimental import pallas as pl
from jax.experimental.pallas import tpu as pltpu

F32 = jnp.float32
BF16 = jnp.bfloat16

D_MODEL = 1024
DEPTH = 4
GM_WIDTH = 512
GM_GROUPS = 4
N_Q_HEADS = 8
N_KV_HEADS = 2
HEAD_DIM = 64
ATT_WIDTH = N_Q_HEADS * HEAD_DIM
ROPE_THETA = 10000.0
MASK_VALUE = -1e30
HG_HEADS = 4
HG_EXPAND = 128
HG_KEY_WIDTH = HG_HEADS * HG_EXPAND
HG_WIDTH = HG_HEADS * 128
HG_MIN_FORGET = 1e-6
D_FF = 2816
EPS = 1e-6

LANES = 128
BLK = 128
MIX_SUB = 4
PROJ_SUB = 2
PROJ_DELAY = 2
TM_IN = 256
TM_FFN = 512
FF_CHUNK = 256
VMEM_LIMIT = 56 * 1024 * 1024
WEIGHT_PREP_BLOCK_BYTES = 8 * 1024 * 1024

C_GU, C_GV, C_Q, C_K, C_V, C_HQ, C_HF, C_HI, C_HG, C_GATE, C_END = (
    0, 512, 1024, 1536, 1664, 1792, 2304, 2816, 3328, 3840, 6912)

HG_LEVEL_HALVES = (64, 32, 16, 8)
HG_DIAG = 8


def _hgrn_tables():
    t = np.arange(BLK)
    tri = (t[None, :] <= t[:, None]).astype(np.float32)
    lvl = np.zeros((BLK, BLK), np.int32)
    for li, h in enumerate(HG_LEVEL_HALVES):
        blk = 2 * h
        second = (t % blk) >= h
        same = (t[:, None] // blk) == (t[None, :] // blk)
        lvl[same & second[:, None] & (~second)[None, :]] = li + 1
    same = (t[:, None] // HG_DIAG) == (t[None, :] // HG_DIAG)
    lvl[same & (t[None, :] <= t[:, None])] = len(HG_LEVEL_HALVES) + 1
    return tri, lvl


_HG_TRI_NP, _HG_LVL_NP = _hgrn_tables()


def _dot(a, b):
    return jnp.dot(a, b, preferred_element_type=F32)


def _dot_nt(a, b):
    return lax.dot_general(a, b, (((1,), (1,)), ((), ())), preferred_element_type=F32)


def _sigmoid(x):
    return 1.0 / (1.0 + jnp.exp(-x))


def _silu(x):
    return x * _sigmoid(x)


def _gelu_tanh(x):
    return 0.5 * x * (1.0 + jnp.tanh(np.sqrt(2.0 / np.pi).astype(np.float32) * (x + 0.044715 * (x * x * x))))


def _rms(x):
    return x * lax.rsqrt(jnp.mean(x * x, axis=-1, keepdims=True) + EPS)


def _const_spec(shape):
    nd = len(shape)
    return pl.BlockSpec(shape, lambda b, i: (0,) * nd, pipeline_mode=pl.Buffered(1))


def _layer_spec(arr, layer):
    shape = arr.shape[1:]
    return pl.BlockSpec((None,) + shape, lambda b, i: (layer,) + (0,) * len(shape), pipeline_mode=pl.Buffered(1))


def _mod_spec(layer, nb, part):
    return pl.BlockSpec((1, 1, D_MODEL), lambda b, i: ((layer * nb + b) * 6 + part, 0, 0))


def _weight_kernel(w_ref, o_ref):
    n = w_ref.shape[1]
    o_ref[:, 0:n] = w_ref[...].astype(BF16)
    if n < o_ref.shape[1]:
        o_ref[:, n:] = jnp.zeros((o_ref.shape[0], o_ref.shape[1] - n), BF16)


def _mxu_weight(w):
    depth, k, n = w.shape
    width = n + LANES if (n // LANES) % 2 == 0 else n
    tk = max(t for t in (256, 512, 1024, 1408) if k % t == 0 and t * n * 4 <= WEIGHT_PREP_BLOCK_BYTES)
    return pl.pallas_call(
        _weight_kernel,
        grid=(depth, k // tk),
        in_specs=[pl.BlockSpec((None, tk, n), lambda l, i: (l, i, 0))],
        out_specs=pl.BlockSpec((None, tk, width), lambda l, i: (l, i, 0)),
        out_shape=jax.ShapeDtypeStruct((depth, k, width), BF16),
        compiler_params=pltpu.CompilerParams(
            dimension_semantics=("parallel", "parallel"), vmem_limit_bytes=VMEM_LIMIT),
        name="weight_prep",
    )(w)


def _rows(v):
    return v.reshape(v.shape[0], 1, -1)


def _ada_kernel(c_ref, w_ref, b_ref, o_ref):
    c = c_ref[...]
    o_ref[0] = _dot(_silu(c).astype(BF16), w_ref[0].astype(BF16)) + b_ref[0]


def _ada_mod(c, ada_w, ada_b):
    nb = c.shape[0]
    ncol = ada_w.shape[2] // D_MODEL
    return pl.pallas_call(
        _ada_kernel,
        grid=(DEPTH, ncol),
        in_specs=[pl.BlockSpec((nb, D_MODEL), lambda l, j: (0, 0)),
                  pl.BlockSpec((1, D_MODEL, D_MODEL), lambda l, j: (l, 0, j)),
                  pl.BlockSpec((1, 1, D_MODEL), lambda l, j: (l, 0, j))],
        out_specs=pl.BlockSpec((1, nb, D_MODEL), lambda l, j: (l, 0, j)),
        out_shape=jax.ShapeDtypeStruct((DEPTH, nb, ncol * D_MODEL), F32),
        name="ada_mod",
    )(c, ada_w, ada_b.reshape(DEPTH, 1, -1))


def _lb_kernel(x_ref, o_ref):
    x = x_ref[...]
    e = jnp.exp(x - jnp.max(x, axis=0, keepdims=True))
    p = e / jnp.sum(e, axis=0, keepdims=True)
    run = jnp.zeros_like(p[0:1])
    rows = []
    for l in range(DEPTH):
        run = run + p[l:l + 1]
        rows.append(run - p[0:1])
    o_ref[...] = jnp.concatenate(rows, axis=0)


def _forget_lower_bounds(logits):
    return pl.pallas_call(
        _lb_kernel, out_shape=jax.ShapeDtypeStruct(logits.shape, F32), name="hgrn_lower_bounds",
    )(logits.astype(F32))


def _rope_kernel(pos_ref, inv_ref, sign_ref, cos_ref, sin_ref):
    ang = pos_ref[0] * inv_ref[...]
    cos_ref[0] = jnp.cos(ang)
    sin_ref[0] = jnp.sin(ang) * sign_ref[...]


def _rope_tables(positions):
    nb, s = positions.shape
    half = HEAD_DIM // 2
    inv = ROPE_THETA ** (-jnp.arange(half, dtype=F32) / half)
    inv_t = jnp.tile(inv, LANES // half)[None, :]
    sign = np.where((np.arange(LANES) % HEAD_DIM) < half, -1.0, 1.0).astype(np.float32)[None, :]
    ts = 512
    tab = jax.ShapeDtypeStruct((nb, s, LANES), F32)
    return pl.pallas_call(
        _rope_kernel,
        grid=(nb, s // ts),
        in_specs=[pl.BlockSpec((1, ts, 1), lambda b, i: (b, i, 0)),
                  pl.BlockSpec((1, LANES), lambda b, i: (0, 0)),
                  pl.BlockSpec((1, LANES), lambda b, i: (0, 0))],
        out_specs=[pl.BlockSpec((1, ts, LANES), lambda b, i: (b, i, 0))] * 2,
        out_shape=[tab, tab],
        name="rope_tables",
    )(positions.astype(F32)[..., None], inv_t, jnp.asarray(sign))


def _rope_apply(x, cos, sin_signed, first_half):
    partner = jnp.where(first_half, pltpu.roll(x, LANES - HEAD_DIM // 2, 1), pltpu.roll(x, HEAD_DIM // 2, 1))
    return x * cos + partner * sin_signed


def _in_kernel(x_ref, sh_ref, sc_ref, g_ref, w_ref, cos_ref, sin_ref, lb_ref, lng_ref, lnb_ref,
               uv_ref, qkv_ref, hg_ref, lf_ref, gate_ref):
    x = x_ref[0]
    h = _rms(x) * g_ref[...]
    h = h * (1.0 + sc_ref[0]) + sh_ref[0]
    hb = h.astype(BF16)

    def proj(a, b):
        return _dot(hb, w_ref[:, a:b])

    cos = cos_ref[0]
    sin = sin_ref[0]
    lane = lax.broadcasted_iota(jnp.int32, cos.shape, 1)
    first_half = (lane % HEAD_DIM) < (HEAD_DIM // 2)
    lb = lb_ref[...]

    def epi_u(d):
        uv_ref[0, :, 0:GM_WIDTH] = _gelu_tanh(d).astype(BF16)

    def epi_v(d):
        v = _gelu_tanh(d)
        mu = jnp.mean(v, axis=-1, keepdims=True)
        vc = v - mu
        var = jnp.mean(vc * vc, axis=-1, keepdims=True)
        uv_ref[0, :, GM_WIDTH:2 * GM_WIDTH] = (vc * lax.rsqrt(var + EPS) * lng_ref[...] + lnb_ref[...]).astype(BF16)

    def epi_q(d):
        for gidx in range(d.shape[1] // LANES):
            r = _rope_apply(d[:, gidx * LANES:(gidx + 1) * LANES], cos, sin, first_half) * (HEAD_DIM ** -0.5)
            qkv_ref[0, :, gidx * LANES:(gidx + 1) * LANES] = r.astype(BF16)

    def epi_kv(d):
        low = (lane % LANES) < HEAD_DIM
        for t, x in enumerate((_rope_apply(d[:, 0:LANES], cos, sin, first_half), d[:, LANES:2 * LANES])):
            swapped = pltpu.roll(x, HEAD_DIM, 1)
            base = ATT_WIDTH + t * 2 * LANES
            qkv_ref[0, :, base:base + LANES] = jnp.where(low, x, swapped).astype(BF16)
            qkv_ref[0, :, base + LANES:base + 2 * LANES] = jnp.where(low, swapped, x).astype(BF16)

    def epi_hq(d):
        hg_ref[0, :, 0:512] = (_silu(d) * (HG_EXPAND ** -0.5)).astype(BF16)

    def epi_hf(fl):
        forget = lb + (1.0 - lb) * _sigmoid(fl)
        lf_ref[0] = jnp.log(jnp.maximum(forget, HG_MIN_FORGET))
        hg_ref[0, :, 512:1024] = ((1.0 - lb) * _sigmoid(-fl)).astype(BF16)

    def epi_hi(d):
        hg_ref[0, :, 1024:1536] = d.astype(BF16)

    def epi_hgate(d):
        hg_ref[0, :, 1536:2048] = _silu(d).astype(BF16)

    def epi_gate(gidx):
        def f(d):
            gate_ref[0, :, gidx * 512:(gidx + 1) * 512] = _sigmoid(d).astype(BF16)
        return f

    def gate_stage(g):
        return (C_GATE + g * 512, C_GATE + (g + 1) * 512, epi_gate(g))

    stages = [(C_GU, C_GV, epi_u), gate_stage(0), (C_GV, C_Q, epi_v), gate_stage(1), gate_stage(2),
              (C_Q, C_K, epi_q), gate_stage(3), (C_HQ, C_HF, epi_hq), gate_stage(4), (C_HF, C_HI, epi_hf),
              gate_stage(5), (C_HG, C_GATE, epi_hgate), (C_K, C_HQ, epi_kv), (C_HI, C_HG, epi_hi)]
    for a, b, epi in stages:
        epi(proj(a, b))


def _in_call(layer, x, mod, g, w, cos_t, sin_t, lb, lng, lnb):
    nb, s, _ = x.shape
    tm = TM_IN

    def tok(width):
        return pl.BlockSpec((1, tm, width), lambda b, i: (b, i, 0))

    def out(width, dt):
        return jax.ShapeDtypeStruct((nb, s, width), dt)

    return pl.pallas_call(
        _in_kernel,
        grid=(nb, s // tm),
        in_specs=[tok(D_MODEL), _mod_spec(layer, nb, 0), _mod_spec(layer, nb, 1), _layer_spec(g, layer),
                  _layer_spec(w, layer), tok(LANES), tok(LANES),
                  _layer_spec(lb, layer), _layer_spec(lng, layer), _layer_spec(lnb, layer)],
        out_specs=[tok(1024), tok(1024), tok(2048), tok(512), tok(3072)],
        out_shape=[out(1024, BF16), out(1024, BF16), out(2048, BF16), out(512, F32), out(3072, BF16)],
        compiler_params=pltpu.CompilerParams(
            dimension_semantics=("parallel", "parallel"), vmem_limit_bytes=VMEM_LIMIT),
        name="mixer_in",
    )(x, mod, mod, g, w, cos_t, sin_t, lb, lng, lnb)


def _spatial_gating(uv, ws_bf, bs_ref):
    outs = []
    for g in range(GM_GROUPS):
        sl = slice(g * LANES, (g + 1) * LANES)
        s = _dot(ws_bf[g], uv[:, GM_WIDTH + g * LANES:GM_WIDTH + (g + 1) * LANES]) + bs_ref[g]
        outs.append(uv[:, sl].astype(F32) * s)
    return jnp.concatenate(outs, axis=1)


def _interleave(chains):
    live = list(chains)
    rnd = 0
    while live:
        for item in list(live):
            if item[0] <= rnd:
                try:
                    next(item[1])
                except StopIteration:
                    live.remove(item)
        rnd += 1


def _attention_kv_head(j, qkv, k_prev, v_prev, sink_ref, sink_base, first_key, out):
    nk = 2 * BLK
    qi = lax.broadcasted_iota(jnp.int32, (BLK, nk), 0)
    kj = lax.broadcasted_iota(jnp.int32, (BLK, nk), 1)
    valid = (kj > qi) & (kj <= qi + BLK) & (kj >= first_key)
    row_t = lax.broadcasted_iota(jnp.int32, (LANES, nk), 0)
    low_t = row_t < HEAD_DIM
    lane_o = lax.broadcasted_iota(jnp.int32, (BLK, LANES), 1)
    low_o = lane_o < HEAD_DIM
    pairs_per_kv = (N_Q_HEADS // N_KV_HEADS) // 2
    k_cur = qkv[:, ATT_WIDTH + j * LANES:ATT_WIDTH + (j + 1) * LANES]
    v_cur = qkv[:, ATT_WIDTH + (N_KV_HEADS + j) * LANES:ATT_WIDTH + (N_KV_HEADS + j + 1) * LANES]
    kt = jnp.concatenate([k_prev, k_cur], axis=0).T
    zero = jnp.zeros_like(kt)
    k_sel = jnp.concatenate([jnp.where(low_t, kt, zero), jnp.where(low_t, zero, kt)], axis=1)
    vd = jnp.concatenate([jnp.concatenate([v_prev, v_cur], axis=0), jnp.ones((nk, LANES), BF16)], axis=1)
    q4 = jnp.concatenate([qkv[:, (j * pairs_per_kv + pp) * LANES:(j * pairs_per_kv + pp + 1) * LANES]
                          for pp in range(pairs_per_kv)], axis=0)
    s_all = _dot(q4, k_sel)
    yield
    probs, sink_terms = [], []
    for pp in range(pairs_per_kv):
        for par in range(2):
            sink = sink_ref[sink_base + 2 * (j * pairs_per_kv + pp) + par]
            s = jnp.where(valid, s_all[pp * BLK:(pp + 1) * BLK, par * nk:(par + 1) * nk], MASK_VALUE)
            m = jnp.maximum(jnp.max(s, axis=-1, keepdims=True), sink)
            probs.append(jnp.exp(s - m).astype(BF16))
            sink_terms.append(jnp.exp(sink - m))
    o_all = _dot(jnp.concatenate(probs, axis=0), vd)
    yield
    outs = []
    for pp in range(pairs_per_kv):
        halves = []
        for par in range(2):
            blk = o_all[(2 * pp + par) * BLK:(2 * pp + par + 1) * BLK]
            halves.append(blk[:, :LANES] / (blk[:, LANES:] + sink_terms[2 * pp + par]))
        outs.append(jnp.where(low_o, halves[0], halves[1]))
    out["y"] = jnp.concatenate(outs, axis=1)


def _level_exponents(b, h):
    pieces = []
    for k in range(BLK // (2 * h)):
        blk = b[k * 2 * h:(k + 1) * 2 * h]
        r = blk[h - 1:h]
        pieces.append(r - blk[:h])
        pieces.append(blk[h:] - r)
    return jnp.concatenate(pieces, axis=0)


def _block_diag(x2, top_mask):
    zero = jnp.zeros_like(x2)
    return jnp.concatenate([jnp.where(top_mask, x2, zero), jnp.where(top_mask, zero, x2)], axis=0)


def _block_diag_t(x2):
    xt = x2.astype(BF16).T
    row = lax.broadcasted_iota(jnp.int32, xt.shape, 0)
    top = row < LANES
    zero = jnp.zeros_like(xt)
    return jnp.concatenate([jnp.where(top, xt, zero), jnp.where(top, zero, xt)], axis=1)


def _hgrn2_pair(q, k, v, gs, lf, prev, tri2, lvl2, gn2, out):
    n_lv = len(HG_LEVEL_HALVES)
    w2 = 2 * LANES
    lane = lax.broadcasted_iota(jnp.int32, (BLK, w2), 1)
    first = lane < LANES
    lane_s = lax.broadcasted_iota(jnp.int32, (w2, w2), 1)
    row_s = lax.broadcasted_iota(jnp.int32, (w2, w2), 0)
    diag_blocks = (lane_s < LANES) == (row_s < LANES)
    hi = lf.astype(BF16)
    lo = (lf - hi.astype(F32)).astype(BF16)
    b = _dot(tri2, jnp.concatenate([hi, lo], axis=0))
    yield
    st = prev["st"]
    o = _dot_nt((q * jnp.exp(b)).astype(BF16), st.astype(BF16))
    att = jnp.zeros((BLK, w2), F32)
    for li, h in enumerate(HG_LEVEL_HALVES):
        e = jnp.exp(_level_exponents(b, h))
        a = _dot((q * e).astype(BF16), _block_diag_t(k * e))
        att = jnp.where(lvl2 == li + 1, a, att)
    b3 = b.reshape(BLK // HG_DIAG, HG_DIAG, w2)
    xd = (b3 - b3[:, HG_DIAG // 2 - 1:HG_DIAG // 2, :]).reshape(BLK, w2)
    a = _dot((q * jnp.exp(xd)).astype(BF16), _block_diag_t(k * jnp.exp(-xd)))
    att = jnp.where(lvl2 == n_lv + 1, a, att)
    b_last = b[BLK - 1:BLK]
    k_end = (k * jnp.exp(b_last - b)).astype(BF16)
    yield
    o = o + _dot(att.astype(BF16), _block_diag(v, first))
    out["st"] = jnp.where(diag_blocks, st * jnp.exp(b_last) + _dot(v.T, k_end), 0.0)
    yield
    outs = [_rms(o[:, t * LANES:(t + 1) * LANES]) for t in range(2)]
    out["y"] = jnp.concatenate(outs, axis=1) * gn2 * gs


def _mix_kernel(sink_ref, x_ref, uv_ref, qkv_ref, hg_ref, lf_ref, gate_ref, g1_ref, gpost_ref,
                ws_ref, bs_ref, tri_ref, lvl_ref, gn_ref, pa_ref, pb_ref, pc_ref, wo_ref,
                o_ref, kprev_ref, vprev_ref, state_ref, *, layer):
    step = pl.program_id(1)

    @pl.when(step == 0)
    def _():
        kprev_ref[...] = jnp.zeros_like(kprev_ref)
        vprev_ref[...] = jnp.zeros_like(vprev_ref)
        state_ref[...] = jnp.zeros_like(state_ref)

    row = lax.broadcasted_iota(jnp.int32, (BLK, BLK), 0)
    col = lax.broadcasted_iota(jnp.int32, (BLK, BLK), 1)
    ws_bf = [jnp.where(col <= row, ws_ref[g], 0.0).astype(BF16) for g in range(GM_GROUPS)]
    tri2 = tri_ref[...]
    lvl2 = lvl_ref[...]
    gn2 = gn_ref[...]
    n_pair = HG_HEADS // 2
    k_off = ATT_WIDTH
    v_off = ATT_WIDTH + N_KV_HEADS * LANES
    att_out = [[dict() for _ in range(N_KV_HEADS)] for _ in range(MIX_SUB)]
    hg_out = [[dict() for _ in range(n_pair)] for _ in range(MIX_SUB)]
    hg_init = [{"st": state_ref[p]} for p in range(n_pair)]
    chains = []
    for sb in range(MIX_SUB):
        rows = slice(sb * BLK, (sb + 1) * BLK)
        first_key = jnp.where(step > 0, 0, BLK) if sb == 0 else 0
        qkv = qkv_ref[0, rows, :]
        hg = hg_ref[0, rows, :]
        lf = lf_ref[0, rows, :]
        for j in range(N_KV_HEADS):
            if sb == 0:
                kp, vp = kprev_ref[j], vprev_ref[j]
            else:
                prow = slice((sb - 1) * BLK, sb * BLK)
                kp = qkv_ref[0, prow, k_off + j * LANES:k_off + (j + 1) * LANES]
                vp = qkv_ref[0, prow, v_off + j * LANES:v_off + (j + 1) * LANES]
            chains.append((sb, _attention_kv_head(j, qkv, kp, vp, sink_ref, layer * N_Q_HEADS, first_key,
                                                  att_out[sb][j])))
        for p in range(n_pair):
            def part(base):
                return hg[:, base + p * 2 * LANES:base + (p + 1) * 2 * LANES]
            prev = hg_init[p] if sb == 0 else hg_out[sb - 1][p]
            chains.append((sb, _hgrn2_pair(
                part(0).astype(F32), part(512).astype(F32), part(1024), part(1536).astype(F32),
                lf[:, p * 2 * LANES:(p + 1) * 2 * LANES], prev, tri2, lvl2, gn2, hg_out[sb][p])))

    def projection(g):
        sbs = range(g * PROJ_SUB, (g + 1) * PROJ_SUB)
        grows = slice(g * PROJ_SUB * BLK, (g + 1) * PROJ_SUB * BLK)
        y_a = jnp.concatenate([_spatial_gating(uv_ref[0, sb * BLK:(sb + 1) * BLK, :], ws_bf, bs_ref)
                               for sb in sbs], axis=0)
        gates = gate_ref[0, grows, :]
        merged = gates[:, 0:D_MODEL].astype(F32) * _dot(y_a.astype(BF16), pa_ref[:, 0:D_MODEL])
        yield
        y_b = jnp.concatenate([jnp.concatenate([att_out[sb][j]["y"] for j in range(N_KV_HEADS)], axis=1)
                               for sb in sbs], axis=0)
        merged = merged + gates[:, D_MODEL:2 * D_MODEL].astype(F32) * _dot(y_b.astype(BF16), pb_ref[:, 0:D_MODEL])
        yield
        y_c = jnp.concatenate([jnp.concatenate([hg_out[sb][p]["y"] for p in range(n_pair)], axis=1)
                               for sb in sbs], axis=0)
        merged = merged + gates[:, 2 * D_MODEL:3 * D_MODEL].astype(F32) * _dot(y_c.astype(BF16), pc_ref[:, 0:D_MODEL])
        yield
        y = _dot(merged.astype(BF16), wo_ref[:, 0:D_MODEL])
        o_ref[0, grows, :] = x_ref[0, grows, :] + g1_ref[0] * (_rms(y) * gpost_ref[...])

    for g in range(MIX_SUB // PROJ_SUB):
        chains.append(((g + 1) * PROJ_SUB - 1 + PROJ_DELAY, projection(g)))
    _interleave(chains)
    last = slice((MIX_SUB - 1) * BLK, MIX_SUB * BLK)
    for j in range(N_KV_HEADS):
        kprev_ref[j] = qkv_ref[0, last, k_off + j * LANES:k_off + (j + 1) * LANES]
        vprev_ref[j] = qkv_ref[0, last, v_off + j * LANES:v_off + (j + 1) * LANES]
    for p in range(n_pair):
        state_ref[p] = hg_out[MIX_SUB - 1][p]["st"]


def _mix_call(layer, sinks, x, uv, qkv, hg, lf, gates, mod, gpost, ws, bs_b, tri, lvl, gn, pa, pb, pc, wo):
    nb, s, _ = x.shape
    tb = MIX_SUB * BLK

    def tok(width):
        return pl.BlockSpec((1, tb, width), lambda b, i: (b, i, 0))

    return pl.pallas_call(
        functools.partial(_mix_kernel, layer=layer),
        grid=(nb, s // tb),
        in_specs=[pl.BlockSpec(memory_space=pltpu.SMEM),
                  tok(D_MODEL), tok(1024), tok(1024), tok(2048), tok(512), tok(3072),
                  _mod_spec(layer, nb, 2), _layer_spec(gpost, layer),
                  _layer_spec(ws, layer), _layer_spec(bs_b, layer),
                  _const_spec((BLK, 2 * BLK)), _const_spec((BLK, 2 * BLK)), _layer_spec(gn, layer),
                  _layer_spec(pa, layer), _layer_spec(pb, layer), _layer_spec(pc, layer), _layer_spec(wo, layer)],
        out_specs=tok(D_MODEL),
        out_shape=jax.ShapeDtypeStruct(x.shape, F32),
        scratch_shapes=[pltpu.VMEM((N_KV_HEADS, BLK, LANES), BF16),
                        pltpu.VMEM((N_KV_HEADS, BLK, LANES), BF16),
                        pltpu.VMEM((HG_HEADS // 2, 2 * LANES, 2 * HG_EXPAND), F32)],
        compiler_params=pltpu.CompilerParams(
            dimension_semantics=("parallel", "arbitrary"), vmem_limit_bytes=VMEM_LIMIT),
        name="mixer_core",
    )(sinks, x, uv, qkv, hg, lf, gates, mod, gpost, ws, bs_b, tri, lvl, gn, pa, pb, pc, wo)


def _ffn_kernel(x_ref, sh_ref, sc_ref, gate_ref, gpre_ref, gpost_ref, w1_ref, cw_ref, cb_ref, w2_ref,
                o_ref, a_ref, hid_ref):
    step = pl.program_id(1)
    tm = x_ref.shape[1]

    @pl.when(step == 0)
    def _():
        a_ref[:, 0:8, :] = jnp.zeros((a_ref.shape[0], 8, LANES), F32)

    x = x_ref[0]
    h = _rms(x) * gpre_ref[...]
    hb = (h * (1.0 + sc_ref[0]) + sh_ref[0]).astype(BF16)

    def conv(col):
        a = _dot(hb, w1_ref[:, col:col + FF_CHUNK])
        taps = []
        for t in range(FF_CHUNK // LANES):
            a_ref[col // LANES + t, 8:8 + tm, :] = a[:, t * LANES:(t + 1) * LANES]
        for t in range(FF_CHUNK // LANES):
            slab = col // LANES + t
            sl = slice(col + t * LANES, col + (t + 1) * LANES)
            cw = cw_ref[:, sl]
            taps.append(cw[0:1] * a_ref[slab, 6:6 + tm, :] + cw[1:2] * a_ref[slab, 7:7 + tm, :]
                        + cw[2:3] * a[:, t * LANES:(t + 1) * LANES] + cb_ref[:, sl])
            a_ref[slab, 0:8, :] = a[tm - 8:tm, t * LANES:(t + 1) * LANES]
        return jnp.concatenate(taps, axis=1)

    for c in range(D_FF // FF_CHUNK):
        gate = conv(c * FF_CHUNK)
        val = conv(D_FF + c * FF_CHUNK)
        hid_ref[:, c * FF_CHUNK:(c + 1) * FF_CHUNK] = (_silu(gate) * val).astype(BF16)
    y = _dot(hid_ref[...], w2_ref[:, 0:D_MODEL])
    o_ref[0] = x + gate_ref[0] * (_rms(y) * gpost_ref[...])


def _ffn_call(layer, x, mod, gpre, gpost, w1, cw, cb, w2):
    nb, s, _ = x.shape
    tm = TM_FFN
    tok = pl.BlockSpec((1, tm, D_MODEL), lambda b, i: (b, i, 0))
    return pl.pallas_call(
        _ffn_kernel,
        grid=(nb, s // tm),
        in_specs=[tok, _mod_spec(layer, nb, 3), _mod_spec(layer, nb, 4), _mod_spec(layer, nb, 5),
                  _layer_spec(gpre, layer), _layer_spec(gpost, layer),
                  _layer_spec(w1, layer), _layer_spec(cw, layer), _layer_spec(cb, layer), _layer_spec(w2, layer)],
        out_specs=tok,
        out_shape=jax.ShapeDtypeStruct(x.shape, F32),
        scratch_shapes=[pltpu.VMEM((2 * D_FF // LANES, 8 + tm, LANES), F32), pltpu.VMEM((tm, D_FF), BF16)],
        compiler_params=pltpu.CompilerParams(
            dimension_semantics=("parallel", "arbitrary"), vmem_limit_bytes=VMEM_LIMIT),
        name="conv_ffn",
    )(x, mod, mod, mod, gpre, gpost, w1, cw, cb, w2)


def kernel(x, c, positions, ada_w, ada_b, norm_mix_pre, norm_mix_post, norm_ffn_pre, norm_ffn_post, w_in,
           gm_ln_g, gm_ln_b, gm_ws, gm_bs, attn_sinks, hg_lb_logits, hg_gnorm, proj_a, proj_b, proj_c, w_out,
           ffn_w1, ffn_conv_w, ffn_conv_b, ffn_w2):
    mod = _ada_mod(c, ada_w, ada_b).reshape(-1, 1, D_MODEL)
    lb_all = _rows(_forget_lower_bounds(hg_lb_logits))
    cos_t, sin_t = _rope_tables(positions)
    tri2 = jnp.asarray(np.tile(_HG_TRI_NP, (1, 2)), dtype=BF16)
    lvl2 = jnp.asarray(np.tile(_HG_LVL_NP, (1, 2)))
    w_in_b = _mxu_weight(w_in)
    pa, pb, pc, wo = (_mxu_weight(t) for t in (proj_a, proj_b, proj_c, w_out))
    w1, w2 = _mxu_weight(ffn_w1), _mxu_weight(ffn_w2)
    bs_b = jnp.broadcast_to(gm_bs[:, :, :, None], (DEPTH, GM_GROUPS, BLK, BLK))
    sinks = attn_sinks.reshape(-1)
    gn2 = _rows(jnp.tile(hg_gnorm, (1, 2)))
    g_mix_pre, g_mix_post = _rows(norm_mix_pre), _rows(norm_mix_post)
    g_ffn_pre, g_ffn_post = _rows(norm_ffn_pre), _rows(norm_ffn_post)
    ln_g, ln_b, conv_b = _rows(gm_ln_g), _rows(gm_ln_b), _rows(ffn_conv_b)

    for l in range(DEPTH):
        uv, qkv, hg, lf, gates = _in_call(l, x, mod, g_mix_pre, w_in_b, cos_t, sin_t, lb_all, ln_g, ln_b)
        x = _mix_call(l, sinks, x, uv, qkv, hg, lf, gates, mod, g_mix_post, gm_ws, bs_b, tri2, lvl2, gn2,
                      pa, pb, pc, wo)
        x = _ffn_call(l, x, mod, g_ffn_pre, g_ffn_post, w1, ffn_conv_w, conv_b, w2)
    return x
```

```python
import functools

import jax
import jax.numpy as jnp
import numpy as np
from jax import lax
from jax.experimental import pallas as pl
from jax.experimental.pallas import tpu as pltpu

F32 = jnp.float32
BF16 = jnp.bfloat16

D_MODEL = 1024
DEPTH = 4
GM_WIDTH = 512
GM_GROUPS = 4
N_Q_HEADS = 8
N_KV_HEADS = 2
HEAD_DIM = 64
ATT_WIDTH = N_Q_HEADS * HEAD_DIM
ROPE_THETA = 10000.0
MASK_VALUE = -1e30
HG_HEADS = 4
HG_EXPAND = 128
HG_KEY_WIDTH = HG_HEADS * HG_EXPAND
HG_WIDTH = HG_HEADS * 128
HG_MIN_FORGET = 1e-6
D_FF = 2816
EPS = 1e-6

LANES = 128
BLK = 128
MIX_SUB = 4
PROJ_SUB = 2
PROJ_DELAY = 2
TM_IN = 256
TM_FFN = 512
FF_CHUNK = 256
VMEM_LIMIT = 56 * 1024 * 1024
WEIGHT_PREP_BLOCK_BYTES = 8 * 1024 * 1024

C_GU, C_GV, C_Q, C_K, C_V, C_HQ, C_HF, C_HI, C_HG, C_GATE, C_END = (
    0, 512, 1024, 1536, 1664, 1792, 2304, 2816, 3328, 3840, 6912)

HG_LEVEL_HALVES = (64, 32, 16, 8)
HG_DIAG = 8


def _hgrn_tables():
    t = np.arange(BLK)
    tri = (t[None, :] <= t[:, None]).astype(np.float32)
    lvl = np.zeros((BLK, BLK), np.int32)
    for li, h in enumerate(HG_LEVEL_HALVES):
        blk = 2 * h
        second = (t % blk) >= h
        same = (t[:, None] // blk) == (t[None, :] // blk)
        lvl[same & second[:, None] & (~second)[None, :]] = li + 1
    same = (t[:, None] // HG_DIAG) == (t[None, :] // HG_DIAG)
    lvl[same & (t[None, :] <= t[:, None])] = len(HG_LEVEL_HALVES) + 1
    return tri, lvl


_HG_TRI_NP, _HG_LVL_NP = _hgrn_tables()


def _dot(a, b):
    return jnp.dot(a, b, preferred_element_type=F32)


def _dot_nt(a, b):
    return lax.dot_general(a, b, (((1,), (1,)), ((), ())), preferred_element_type=F32)


def _sigmoid(x):
    return 1.0 / (1.0 + jnp.exp(-x))


def _silu(x):
    return x * _sigmoid(x)


def _gelu_tanh(x):
    return 0.5 * x * (1.0 + jnp.tanh(np.sqrt(2.0 / np.pi).astype(np.float32) * (x + 0.044715 * (x * x * x))))


def _rms(x):
    return x * lax.rsqrt(jnp.mean(x * x, axis=-1, keepdims=True) + EPS)


def _const_spec(shape):
    nd = len(shape)
    return pl.BlockSpec(shape, lambda b, i: (0,) * nd, pipeline_mode=pl.Buffered(1))


def _layer_spec(arr, layer):
    shape = arr.shape[1:]
    return pl.BlockSpec((None,) + shape, lambda b, i: (layer,) + (0,) * len(shape), pipeline_mode=pl.Buffered(1))


def _mod_spec(layer, nb, part):
    return pl.BlockSpec((1, 1, D_MODEL), lambda b, i: ((layer * nb + b) * 6 + part, 0, 0))


def _weight_kernel(w_ref, o_ref):
    n = w_ref.shape[1]
    o_ref[:, 0:n] = w_ref[...].astype(BF16)
    if n < o_ref.shape[1]:
        o_ref[:, n:] = jnp.zeros((o_ref.shape[0], o_ref.shape[1] - n), BF16)


def _mxu_weight(w):
    depth, k, n = w.shape
    width = n + LANES if (n // LANES) % 2 == 0 else n
    tk = max(t for t in (256, 512, 1024, 1408) if k % t == 0 and t * n * 4 <= WEIGHT_PREP_BLOCK_BYTES)
    return pl.pallas_call(
        _weight_kernel,
        grid=(depth, k // tk),
        in_specs=[pl.BlockSpec((None, tk, n), lambda l, i: (l, i, 0))],
        out_specs=pl.BlockSpec((None, tk, width), lambda l, i: (l, i, 0)),
        out_shape=jax.ShapeDtypeStruct((depth, k, width), BF16),
        compiler_params=pltpu.CompilerParams(
            dimension_semantics=("parallel", "parallel"), vmem_limit_bytes=VMEM_LIMIT),
        name="weight_prep",
    )(w)


def _rows(v):
    return v.reshape(v.shape[0], 1, -1)


def _ada_kernel(c_ref, w_ref, b_ref, o_ref):
    c = c_ref[...]
    o_ref[0] = _dot(_silu(c).astype(BF16), w_ref[0].astype(BF16)) + b_ref[0]


def _ada_mod(c, ada_w, ada_b):
    nb = c.shape[0]
    ncol = ada_w.shape[2] // D_MODEL
    return pl.pallas_call(
        _ada_kernel,
        grid=(DEPTH, ncol),
        in_specs=[pl.BlockSpec((nb, D_MODEL), lambda l, j: (0, 0)),
                  pl.BlockSpec((1, D_MODEL, D_MODEL), lambda l, j: (l, 0, j)),
                  pl.BlockSpec((1, 1, D_MODEL), lambda l, j: (l, 0, j))],
        out_specs=pl.BlockSpec((1, nb, D_MODEL), lambda l, j: (l, 0, j)),
        out_shape=jax.ShapeDtypeStruct((DEPTH, nb, ncol * D_MODEL), F32),
        name="ada_mod",
    )(c, ada_w, ada_b.reshape(DEPTH, 1, -1))


def _lb_kernel(x_ref, o_ref):
    x = x_ref[...]
    e = jnp.exp(x - jnp.max(x, axis=0, keepdims=True))
    p = e / jnp.sum(e, axis=0, keepdims=True)
    run = jnp.zeros_like(p[0:1])
    rows = []
    for l in range(DEPTH):
        run = run + p[l:l + 1]
        rows.append(run - p[0:1])
    o_ref[...] = jnp.concatenate(rows, axis=0)


def _forget_lower_bounds(logits):
    return pl.pallas_call(
        _lb_kernel, out_shape=jax.ShapeDtypeStruct(logits.shape, F32), name="hgrn_lower_bounds",
    )(logits.astype(F32))


def _rope_kernel(pos_ref, inv_ref, sign_ref, cos_ref, sin_ref):
    tok_per_row = LANES // (HEAD_DIM // 2)
    ang = pos_ref[0] * inv_ref[...]
    cos = jnp.cos(ang)
    sin = jnp.sin(ang)
    rows = ang.shape[0]
    for j in range(tok_per_row):
        lanes = slice(j * (HEAD_DIM // 2), (j + 1) * (HEAD_DIM // 2))
        cos_ref[0, pl.ds(j, rows, stride=tok_per_row), :] = jnp.concatenate([cos[:, lanes]] * tok_per_row, axis=1)
        sin_ref[0, pl.ds(j, rows, stride=tok_per_row), :] = (
            jnp.concatenate([sin[:, lanes]] * tok_per_row, axis=1) * sign_ref[...])


def _rope_tables(positions):
    nb, s = positions.shape
    half = HEAD_DIM // 2
    tok_per_row = LANES // half
    inv = ROPE_THETA ** (-jnp.arange(half, dtype=F32) / half)
    inv_t = jnp.tile(inv, tok_per_row)[None, :]
    sign = np.where((np.arange(LANES) % HEAD_DIM) < half, -1.0, 1.0).astype(np.float32)[None, :]
    pos = jnp.repeat(positions.astype(F32).reshape(nb, s // tok_per_row, tok_per_row), half, axis=-1)
    ts = 512
    tab = jax.ShapeDtypeStruct((nb, s, LANES), F32)
    return pl.pallas_call(
        _rope_kernel,
        grid=(nb, s // ts),
        in_specs=[pl.BlockSpec((1, ts // tok_per_row, LANES), lambda b, i: (b, i, 0)),
                  pl.BlockSpec((1, LANES), lambda b, i: (0, 0)),
                  pl.BlockSpec((1, LANES), lambda b, i: (0, 0))],
        out_specs=[pl.BlockSpec((1, ts, LANES), lambda b, i: (b, i, 0))] * 2,
        out_shape=[tab, tab],
        name="rope_tables",
    )(pos, inv_t, jnp.asarray(sign))


def _rope_apply(x, cos, sin_signed, first_half):
    partner = jnp.where(first_half, pltpu.roll(x, LANES - HEAD_DIM // 2, 1), pltpu.roll(x, HEAD_DIM // 2, 1))
    return x * cos + partner * sin_signed


def _in_kernel(x_ref, sh_ref, sc_ref, g_ref, w_ref, cos_ref, sin_ref, lb_ref, lng_ref, lnb_ref,
               uv_ref, qkv_ref, hg_ref, lf_ref, gate_ref):
    x = x_ref[0]
    h = _rms(x) * g_ref[...]
    h = h * (1.0 + sc_ref[0]) + sh_ref[0]
    hb = h.astype(BF16)

    def proj(a, b):
        return _dot(hb, w_ref[:, a:b])

    cos = cos_ref[0]
    sin = sin_ref[0]
    lane = lax.broadcasted_iota(jnp.int32, cos.shape, 1)
    first_half = (lane % HEAD_DIM) < (HEAD_DIM // 2)
    lb = lb_ref[...]

    def epi_u(d):
        uv_ref[0, :, 0:GM_WIDTH] = _gelu_tanh(d).astype(BF16)

    def epi_v(d):
        v = _gelu_tanh(d)
        mu = jnp.mean(v, axis=-1, keepdims=True)
        vc = v - mu
        var = jnp.mean(vc * vc, axis=-1, keepdims=True)
        uv_ref[0, :, GM_WIDTH:2 * GM_WIDTH] = (vc * lax.rsqrt(var + EPS) * lng_ref[...] + lnb_ref[...]).astype(BF16)

    def epi_q(d):
        for gidx in range(d.shape[1] // LANES):
            r = _rope_apply(d[:, gidx * LANES:(gidx + 1) * LANES], cos, sin, first_half) * (HEAD_DIM ** -0.5)
            qkv_ref[0, :, gidx * LANES:(gidx + 1) * LANES] = r.astype(BF16)

    def epi_kv(d):
        low = (lane % LANES) < HEAD_DIM
        for t, x in enumerate((_rope_apply(d[:, 0:LANES], cos, sin, first_half), d[:, LANES:2 * LANES])):
            swapped = pltpu.roll(x, HEAD_DIM, 1)
            base = ATT_WIDTH + t * 2 * LANES
            qkv_ref[0, :, base:base + LANES] = jnp.where(low, x, swapped).astype(BF16)
            qkv_ref[0, :, base + LANES:base + 2 * LANES] = jnp.where(low, swapped, x).astype(BF16)

    def epi_hq(d):
        hg_ref[0, :, 0:512] = (_silu(d) * (HG_EXPAND ** -0.5)).astype(BF16)

    def epi_hf(fl):
        forget = lb + (1.0 - lb) * _sigmoid(fl)
        lf_ref[0] = jnp.log(jnp.maximum(forget, HG_MIN_FORGET))
        hg_ref[0, :, 512:1024] = ((1.0 - lb) * _sigmoid(-fl)).astype(BF16)

    def epi_hi(d):
        hg_ref[0, :, 1024:1536] = d.astype(BF16)

    def epi_hgate(d):
        hg_ref[0, :, 1536:2048] = _silu(d).astype(BF16)

    def epi_gate(gidx):
        def f(d):
            gate_ref[0, :, gidx * 512:(gidx + 1) * 512] = _sigmoid(d).astype(BF16)
        return f

    def gate_stage(g):
        return (C_GATE + g * 512, C_GATE + (g + 1) * 512, epi_gate(g))

    stages = [(C_GU, C_GV, epi_u), gate_stage(0), (C_GV, C_Q, epi_v), gate_stage(1), (C_Q, C_K, epi_q),
              gate_stage(2), (C_HQ, C_HF, epi_hq), gate_stage(3), (C_HF, C_HI, epi_hf), gate_stage(4),
              (C_HG, C_GATE, epi_hgate), gate_stage(5), (C_K, C_HQ, epi_kv), (C_HI, C_HG, epi_hi)]
    for a, b, epi in stages:
        epi(proj(a, b))


def _in_call(layer, x, mod, g, w, cos_t, sin_t, lb, lng, lnb):
    nb, s, _ = x.shape
    tm = TM_IN

    def tok(width):
        return pl.BlockSpec((1, tm, width), lambda b, i: (b, i, 0))

    def out(width, dt):
        return jax.ShapeDtypeStruct((nb, s, width), dt)

    return pl.pallas_call(
        _in_kernel,
        grid=(nb, s // tm),
        in_specs=[tok(D_MODEL), _mod_spec(layer, nb, 0), _mod_spec(layer, nb, 1), _layer_spec(g, layer),
                  _layer_spec(w, layer), tok(LANES), tok(LANES),
                  _layer_spec(lb, layer), _layer_spec(lng, layer), _layer_spec(lnb, layer)],
        out_specs=[tok(1024), tok(1024), tok(2048), tok(512), tok(3072)],
        out_shape=[out(1024, BF16), out(1024, BF16), out(2048, BF16), out(512, F32), out(3072, BF16)],
        compiler_params=pltpu.CompilerParams(
            dimension_semantics=("parallel", "parallel"), vmem_limit_bytes=VMEM_LIMIT),
        name="mixer_in",
    )(x, mod, mod, g, w, cos_t, sin_t, lb, lng, lnb)


def _spatial_gating(uv, ws_bf, bs_ref):
    outs = []
    for g in range(GM_GROUPS):
        sl = slice(g * LANES, (g + 1) * LANES)
        s = _dot(ws_bf[g], uv[:, GM_WIDTH + g * LANES:GM_WIDTH + (g + 1) * LANES]) + bs_ref[g]
        outs.append(uv[:, sl].astype(F32) * s)
    return jnp.concatenate(outs, axis=1)


def _interleave(chains):
    live = list(chains)
    rnd = 0
    while live:
        for item in list(live):
            if item[0] <= rnd:
                try:
                    next(item[1])
                except StopIteration:
                    live.remove(item)
        rnd += 1


def _attention_kv_head(j, qkv, k_prev, v_prev, sink_ref, sink_base, first_key, out):
    nk = 2 * BLK
    qi = lax.broadcasted_iota(jnp.int32, (BLK, nk), 0)
    kj = lax.broadcasted_iota(jnp.int32, (BLK, nk), 1)
    valid = (kj > qi) & (kj <= qi + BLK) & (kj >= first_key)
    row_t = lax.broadcasted_iota(jnp.int32, (LANES, nk), 0)
    low_t = row_t < HEAD_DIM
    lane_o = lax.broadcasted_iota(jnp.int32, (BLK, LANES), 1)
    low_o = lane_o < HEAD_DIM
    pairs_per_kv = (N_Q_HEADS // N_KV_HEADS) // 2
    k_cur = qkv[:, ATT_WIDTH + j * LANES:ATT_WIDTH + (j + 1) * LANES]
    v_cur = qkv[:, ATT_WIDTH + (N_KV_HEADS + j) * LANES:ATT_WIDTH + (N_KV_HEADS + j + 1) * LANES]
    kt = jnp.concatenate([k_prev, k_cur], axis=0).T
    zero = jnp.zeros_like(kt)
    k_sel = jnp.concatenate([jnp.where(low_t, kt, zero), jnp.where(low_t, zero, kt)], axis=1)
    vd = jnp.concatenate([jnp.concatenate([v_prev, v_cur], axis=0), jnp.ones((nk, LANES), BF16)], axis=1)
    q4 = jnp.concatenate([qkv[:, (j * pairs_per_kv + pp) * LANES:(j * pairs_per_kv + pp + 1) * LANES]
                          for pp in range(pairs_per_kv)], axis=0)
    s_all = _dot(q4, k_sel)
    yield
    probs, sink_terms = [], []
    for pp in range(pairs_per_kv):
        for par in range(2):
            sink = sink_ref[sink_base + 2 * (j * pairs_per_kv + pp) + par]
            s = jnp.where(valid, s_all[pp * BLK:(pp + 1) * BLK, par * nk:(par + 1) * nk], MASK_VALUE)
            m = jnp.maximum(jnp.max(s, axis=-1, keepdims=True), sink)
            probs.append(jnp.exp(s - m).astype(BF16))
            sink_terms.append(jnp.exp(sink - m))
    o_all = _dot(jnp.concatenate(probs, axis=0), vd)
    yield
    outs = []
    for pp in range(pairs_per_kv):
        halves = []
        for par in range(2):
            blk = o_all[(2 * pp + par) * BLK:(2 * pp + par + 1) * BLK]
            halves.append(blk[:, :LANES] / (blk[:, LANES:] + sink_terms[2 * pp + par]))
        outs.append(jnp.where(low_o, halves[0], halves[1]))
    out["y"] = jnp.concatenate(outs, axis=1)


def _level_exponents(b, h):
    pieces = []
    for k in range(BLK // (2 * h)):
        blk = b[k * 2 * h:(k + 1) * 2 * h]
        r = blk[h - 1:h]
        pieces.append(r - blk[:h])
        pieces.append(blk[h:] - r)
    return jnp.concatenate(pieces, axis=0)


def _block_diag(x2, top_mask):
    zero = jnp.zeros_like(x2)
    return jnp.concatenate([jnp.where(top_mask, x2, zero), jnp.where(top_mask, zero, x2)], axis=0)


def _block_diag_t(x2):
    xt = x2.astype(BF16).T
    row = lax.broadcasted_iota(jnp.int32, xt.shape, 0)
    top = row < LANES
    zero = jnp.zeros_like(xt)
    return jnp.concatenate([jnp.where(top, xt, zero), jnp.where(top, zero, xt)], axis=1)


def _hgrn2_pair(q, k, v, gs, lf, prev, tri2, lvl2, gn2, out):
    n_lv = len(HG_LEVEL_HALVES)
    w2 = 2 * LANES
    lane = lax.broadcasted_iota(jnp.int32, (BLK, w2), 1)
    first = lane < LANES
    lane_s = lax.broadcasted_iota(jnp.int32, (w2, w2), 1)
    row_s = lax.broadcasted_iota(jnp.int32, (w2, w2), 0)
    diag_blocks = (lane_s < LANES) == (row_s < LANES)
    hi = lf.astype(BF16)
    lo = (lf - hi.astype(F32)).astype(BF16)
    b = _dot(tri2, jnp.concatenate([hi, lo], axis=0))
    yield
    st = prev["st"]
    o = _dot_nt((q * jnp.exp(b)).astype(BF16), st.astype(BF16))
    att = jnp.zeros((BLK, w2), F32)
    for li, h in enumerate(HG_LEVEL_HALVES):
        e = jnp.exp(_level_exponents(b, h))
        a = _dot((q * e).astype(BF16), _block_diag_t(k * e))
        att = jnp.where(lvl2 == li + 1, a, att)
    b3 = b.reshape(BLK // HG_DIAG, HG_DIAG, w2)
    xd = (b3 - b3[:, HG_DIAG // 2 - 1:HG_DIAG // 2, :]).reshape(BLK, w2)
    a = _dot((q * jnp.exp(xd)).astype(BF16), _block_diag_t(k * jnp.exp(-xd)))
    att = jnp.where(lvl2 == n_lv + 1, a, att)
    b_last = b[BLK - 1:BLK]
    k_end = (k * jnp.exp(b_last - b)).astype(BF16)
    yield
    o = o + _dot(att.astype(BF16), _block_diag(v, first))
    out["st"] = jnp.where(diag_blocks, st * jnp.exp(b_last) + _dot(v.T, k_end), 0.0)
    yield
    outs = [_rms(o[:, t * LANES:(t + 1) * LANES]) for t in range(2)]
    out["y"] = jnp.concatenate(outs, axis=1) * gn2 * gs


def _mix_kernel(sink_ref, x_ref, uv_ref, qkv_ref, hg_ref, lf_ref, gate_ref, g1_ref, gpost_ref,
                ws_ref, bs_ref, tri_ref, lvl_ref, gn_ref, pa_ref, pb_ref, pc_ref, wo_ref,
                o_ref, kprev_ref, vprev_ref, state_ref, *, layer):
    step = pl.program_id(1)

    @pl.when(step == 0)
    def _():
        kprev_ref[...] = jnp.zeros_like(kprev_ref)
        vprev_ref[...] = jnp.zeros_like(vprev_ref)
        state_ref[...] = jnp.zeros_like(state_ref)

    row = lax.broadcasted_iota(jnp.int32, (BLK, BLK), 0)
    col = lax.broadcasted_iota(jnp.int32, (BLK, BLK), 1)
    ws_bf = [jnp.where(col <= row, ws_ref[g], 0.0).astype(BF16) for g in range(GM_GROUPS)]
    tri2 = tri_ref[...]
    lvl2 = lvl_ref[...]
    gn2 = gn_ref[...]
    n_pair = HG_HEADS // 2
    k_off = ATT_WIDTH
    v_off = ATT_WIDTH + N_KV_HEADS * LANES
    att_out = [[dict() for _ in range(N_KV_HEADS)] for _ in range(MIX_SUB)]
    hg_out = [[dict() for _ in range(n_pair)] for _ in range(MIX_SUB)]
    hg_init = [{"st": state_ref[p]} for p in range(n_pair)]
    chains = []
    for sb in range(MIX_SUB):
        rows = slice(sb * BLK, (sb + 1) * BLK)
        first_key = jnp.where(step > 0, 0, BLK) if sb == 0 else 0
        qkv = qkv_ref[0, rows, :]
        hg = hg_ref[0, rows, :]
        lf = lf_ref[0, rows, :]
        for j in range(N_KV_HEADS):
            if sb == 0:
                kp, vp = kprev_ref[j], vprev_ref[j]
            else:
                prow = slice((sb - 1) * BLK, sb * BLK)
                kp = qkv_ref[0, prow, k_off + j * LANES:k_off + (j + 1) * LANES]
                vp = qkv_ref[0, prow, v_off + j * LANES:v_off + (j + 1) * LANES]
            chains.append((sb, _attention_kv_head(j, qkv, kp, vp, sink_ref, layer * N_Q_HEADS, first_key,
                                                  att_out[sb][j])))
        for p in range(n_pair):
            def part(base):
                return hg[:, base + p * 2 * LANES:base + (p + 1) * 2 * LANES]
            prev = hg_init[p] if sb == 0 else hg_out[sb - 1][p]
            chains.append((sb, _hgrn2_pair(
                part(0).astype(F32), part(512).astype(F32), part(1024), part(1536).astype(F32),
                lf[:, p * 2 * LANES:(p + 1) * 2 * LANES], prev, tri2, lvl2, gn2, hg_out[sb][p])))

    def projection(g):
        sbs = range(g * PROJ_SUB, (g + 1) * PROJ_SUB)
        grows = slice(g * PROJ_SUB * BLK, (g + 1) * PROJ_SUB * BLK)
        y_a = jnp.concatenate([_spatial_gating(uv_ref[0, sb * BLK:(sb + 1) * BLK, :], ws_bf, bs_ref)
                               for sb in sbs], axis=0)
        gates = gate_ref[0, grows, :]
        merged = gates[:, 0:D_MODEL].astype(F32) * _dot(y_a.astype(BF16), pa_ref[:, 0:D_MODEL])
        yield
        y_b = jnp.concatenate([jnp.concatenate([att_out[sb][j]["y"] for j in range(N_KV_HEADS)], axis=1)
                               for sb in sbs], axis=0)
        merged = merged + gates[:, D_MODEL:2 * D_MODEL].astype(F32) * _dot(y_b.astype(BF16), pb_ref[:, 0:D_MODEL])
        yield
        y_c = jnp.concatenate([jnp.concatenate([hg_out[sb][p]["y"] for p in range(n_pair)], axis=1)
                               for sb in sbs], axis=0)
        merged = merged + gates[:, 2 * D_MODEL:3 * D_MODEL].astype(F32) * _dot(y_c.astype(BF16), pc_ref[:, 0:D_MODEL])
        yield
        y = _dot(merged.astype(BF16), wo_ref[:, 0:D_MODEL])
        o_ref[0, grows, :] = x_ref[0, grows, :] + g1_ref[0] * (_rms(y) * gpost_ref[...])

    for g in range(MIX_SUB // PROJ_SUB):
        chains.append(((g + 1) * PROJ_SUB - 1 + PROJ_DELAY, projection(g)))
    _interleave(chains)
    last = slice((MIX_SUB - 1) * BLK, MIX_SUB * BLK)
    for j in range(N_KV_HEADS):
        kprev_ref[j] = qkv_ref[0, last, k_off + j * LANES:k_off + (j + 1) * LANES]
        vprev_ref[j] = qkv_ref[0, last, v_off + j * LANES:v_off + (j + 1) * LANES]
    for p in range(n_pair):
        state_ref[p] = hg_out[MIX_SUB - 1][p]["st"]


def _mix_call(layer, sinks, x, uv, qkv, hg, lf, gates, mod, gpost, ws, bs_b, tri, lvl, gn, pa, pb, pc, wo):
    nb, s, _ = x.shape
    tb = MIX_SUB * BLK

    def tok(width):
        return pl.BlockSpec((1, tb, width), lambda b, i: (b, i, 0))

    return pl.pallas_call(
        functools.partial(_mix_kernel, layer=layer),
        grid=(nb, s // tb),
        in_specs=[pl.BlockSpec(memory_space=pltpu.SMEM),
                  tok(D_MODEL), tok(1024), tok(1024), tok(2048), tok(512), tok(3072),
                  _mod_spec(layer, nb, 2), _layer_spec(gpost, layer),
                  _layer_spec(ws, layer), _layer_spec(bs_b, layer),
                  _const_spec((BLK, 2 * BLK)), _const_spec((BLK, 2 * BLK)), _layer_spec(gn, layer),
                  _layer_spec(pa, layer), _layer_spec(pb, layer), _layer_spec(pc, layer), _layer_spec(wo, layer)],
        out_specs=tok(D_MODEL),
        out_shape=jax.ShapeDtypeStruct(x.shape, F32),
        scratch_shapes=[pltpu.VMEM((N_KV_HEADS, BLK, LANES), BF16),
                        pltpu.VMEM((N_KV_HEADS, BLK, LANES), BF16),
                        pltpu.VMEM((HG_HEADS // 2, 2 * LANES, 2 * HG_EXPAND), F32)],
        compiler_params=pltpu.CompilerParams(
            dimension_semantics=("parallel", "arbitrary"), vmem_limit_bytes=VMEM_LIMIT),
        name="mixer_core",
    )(sinks, x, uv, qkv, hg, lf, gates, mod, gpost, ws, bs_b, tri, lvl, gn, pa, pb, pc, wo)


def _ffn_kernel(x_ref, sh_ref, sc_ref, gate_ref, gpre_ref, gpost_ref, w1_ref, cw_ref, cb_ref, w2_ref,
                o_ref, a_ref, hid_ref):
    step = pl.program_id(1)
    tm = x_ref.shape[1]

    @pl.when(step == 0)
    def _():
        a_ref[:, 0:8, :] = jnp.zeros((a_ref.shape[0], 8, LANES), F32)

    x = x_ref[0]
    h = _rms(x) * gpre_ref[...]
    hb = (h * (1.0 + sc_ref[0]) + sh_ref[0]).astype(BF16)

    def conv(col):
        a = _dot(hb, w1_ref[:, col:col + FF_CHUNK])
        taps = []
        for t in range(FF_CHUNK // LANES):
            a_ref[col // LANES + t, 8:8 + tm, :] = a[:, t * LANES:(t + 1) * LANES]
        for t in range(FF_CHUNK // LANES):
            slab = col // LANES + t
            sl = slice(col + t * LANES, col + (t + 1) * LANES)
            cw = cw_ref[:, sl]
            taps.append(cw[0:1] * a_ref[slab, 6:6 + tm, :] + cw[1:2] * a_ref[slab, 7:7 + tm, :]
                        + cw[2:3] * a[:, t * LANES:(t + 1) * LANES] + cb_ref[:, sl])
            a_ref[slab, 0:8, :] = a[tm - 8:tm, t * LANES:(t + 1) * LANES]
        return jnp.concatenate(taps, axis=1)

    for c in range(D_FF // FF_CHUNK):
        gate = conv(c * FF_CHUNK)
        val = conv(D_FF + c * FF_CHUNK)
        hid_ref[:, c * FF_CHUNK:(c + 1) * FF_CHUNK] = (_silu(gate) * val).astype(BF16)
    for r in range(2):
        rows = slice(r * tm // 2, (r + 1) * tm // 2)
        y = _dot(hid_ref[rows, :], w2_ref[:, 0:D_MODEL])
        o_ref[0, rows, :] = x[rows] + gate_ref[0] * (_rms(y) * gpost_ref[...])


def _ffn_call(layer, x, mod, gpre, gpost, w1, cw, cb, w2):
    nb, s, _ = x.shape
    tm = TM_FFN
    tok = pl.BlockSpec((1, tm, D_MODEL), lambda b, i: (b, i, 0))
    return pl.pallas_call(
        _ffn_kernel,
        grid=(nb, s // tm),
        in_specs=[tok, _mod_spec(layer, nb, 3), _mod_spec(layer, nb, 4), _mod_spec(layer, nb, 5),
                  _layer_spec(gpre, layer), _layer_spec(gpost, layer),
                  _layer_spec(w1, layer), _layer_spec(cw, layer), _layer_spec(cb, layer), _layer_spec(w2, layer)],
        out_specs=tok,
        out_shape=jax.ShapeDtypeStruct(x.shape, F32),
        scratch_shapes=[pltpu.VMEM((2 * D_FF // LANES, 8 + tm, LANES), F32), pltpu.VMEM((tm, D_FF), BF16)],
        compiler_params=pltpu.CompilerParams(
            dimension_semantics=("parallel", "arbitrary"), vmem_limit_bytes=VMEM_LIMIT),
        name="conv_ffn",
    )(x, mod, mod, mod, gpre, gpost, w1, cw, cb, w2)


def kernel(x, c, positions, ada_w, ada_b, norm_mix_pre, norm_mix_post, norm_ffn_pre, norm_ffn_post, w_in,
           gm_ln_g, gm_ln_b, gm_ws, gm_bs, attn_sinks, hg_lb_logits, hg_gnorm, proj_a, proj_b, proj_c, w_out,
           ffn_w1, ffn_conv_w, ffn_conv_b, ffn_w2):
    mod = _ada_mod(c, ada_w, ada_b).reshape(-1, 1, D_MODEL)
    lb_all = _rows(_forget_lower_bounds(hg_lb_logits))
    cos_t, sin_t = _rope_tables(positions)
    tri2 = jnp.asarray(np.tile(_HG_TRI_NP, (1, 2)), dtype=BF16)
    lvl2 = jnp.asarray(np.tile(_HG_LVL_NP, (1, 2)))
    w_in_b = _mxu_weight(w_in)
    pa, pb, pc, wo = (_mxu_weight(t) for t in (proj_a, proj_b, proj_c, w_out))
    w1, w2 = _mxu_weight(ffn_w1), _mxu_weight(ffn_w2)
    bs_b = jnp.broadcast_to(gm_bs[:, :, :, None], (DEPTH, GM_GROUPS, BLK, BLK))
    sinks = attn_sinks.reshape(-1)
    gn2 = _rows(jnp.tile(hg_gnorm, (1, 2)))
    g_mix_pre, g_mix_post = _rows(norm_mix_pre), _rows(norm_mix_post)
    g_ffn_pre, g_ffn_post = _rows(norm_ffn_pre), _rows(norm_ffn_post)
    ln_g, ln_b, conv_b = _rows(gm_ln_g), _rows(gm_ln_b), _rows(ffn_conv_b)

    for l in range(DEPTH):
        uv, qkv, hg, lf, gates = _in_call(l, x, mod, g_mix_pre, w_in_b, cos_t, sin_t, lb_all, ln_g, ln_b)
        x = _mix_call(l, sinks, x, uv, qkv, hg, lf, gates, mod, g_mix_post, gm_ws, bs_b, tri2, lvl2, gn2,
                      pa, pb, pc, wo)
        x = _ffn_call(l, x, mod, g_ffn_pre, g_ffn_post, w1, ffn_conv_w, conv_b, w2)
    return x
```
